```python
import math
import jax
import jax.numpy as jnp
from jax import lax
import numpy as np

D_MODEL = 4096
BATCH = 4
SEQ = 2048
DEPTH = 2
DEC_BATCH = 128
DEC_SEQ = 4
PAST_LEN = 16384
PAGE_SIZE = 128

N_A_LAYERS = max(1, DEPTH // 2)
N_B_LAYERS = DEPTH - N_A_LAYERS
R_HEADS = 16
R_DK = D_MODEL // R_HEADS
R_DV = 2 * D_MODEL // R_HEADS
R_CHUNK = 128
M_HEADS = D_MODEL // 128
M_NOPE = 128
M_ROPE = 64
M_DV = 128
KV_LORA = 512
Q_LORA = D_MODEL // 4
Q_BLOCK = 128
EPS = 1e-6
M_SCALE = 1.0 / math.sqrt(M_NOPE + M_ROPE)
LOG_ROPE_BASE = math.log(10000.0)

kernel_name = 'yoco_retention_mla_adaln_step'


def rmsnorm(x):
    xf = x.astype(jnp.float32)
    return (xf * lax.rsqrt(jnp.mean(xf * xf, axis=-1, keepdims=True) + EPS)).astype(x.dtype)


def adaln(c, w, b, n):
    return jnp.split(jax.nn.silu(c) @ w + b, n, axis=-1)


def modulated_norm(x, shift, scale):
    return rmsnorm(x) * (1.0 + scale[:, None, :]) + shift[:, None, :]


def rope(x, pos):
    half = x.shape[-1] // 2
    inv = jnp.exp(-LOG_ROPE_BASE * jnp.arange(half, dtype=jnp.float32) / half)
    ang = pos[:, None] * inv[None, :]
    cos = jnp.cos(ang)[None, :, None, :]
    sin = jnp.sin(ang)[None, :, None, :]
    xf = x.astype(jnp.float32)
    x1, x2 = xf[..., :half], xf[..., half:]
    return jnp.concatenate([x1 * cos - x2 * sin, x2 * cos + x1 * sin], axis=-1).astype(x.dtype)


def retention_log_decay():
    return jnp.log1p(-jnp.exp2(-5.0 - jnp.arange(R_HEADS, dtype=jnp.float32)))


def retention_chunk(S, q, k, v, log_g):
    C = q.shape[1]
    idx = jnp.arange(C, dtype=jnp.float32)
    diff = idx[:, None] - idx[None, :]
    decay = jnp.where(diff >= 0, jnp.exp(log_g[:, None, None] * jnp.maximum(diff, 0.0)), 0.0)
    scores = jnp.einsum('bihd,bjhd->bhij', q, k) * decay[None]
    o_intra = jnp.einsum('bhij,bjhv->bihv', scores, v)
    q_decay = jnp.exp(log_g[None, :] * (idx[:, None] + 1.0))
    o_cross = jnp.einsum('bihd,bhdv->bihv', q, S) * q_decay[None, :, :, None]
    k_decay = jnp.exp(log_g[None, :] * (C - 1.0 - idx[:, None]))
    S_new = jnp.exp(log_g * C)[None, :, None, None] * S + jnp.einsum('bjhd,bjhv->bhdv', k * k_decay[None, :, :, None], v)
    return S_new, o_intra + o_cross


def retention_layer(x, c, S0, pos, w_mod, b_mod, w_in, gn_g, w_out):
    B, T, _ = x.shape
    shift, scale, gate = adaln(c, w_mod, b_mod, 3)
    xn = modulated_norm(x, shift, scale)
    proj = xn @ w_in
    hk = R_HEADS * R_DK
    hv = R_HEADS * R_DV
    q = proj[..., :hk].reshape(B, T, R_HEADS, R_DK)
    k = proj[..., hk:2 * hk].reshape(B, T, R_HEADS, R_DK)
    v = proj[..., 2 * hk:2 * hk + hv].reshape(B, T, R_HEADS, R_DV).astype(jnp.float32)
    g = proj[..., 2 * hk + hv:]
    q = rope(q, pos).astype(jnp.float32) * (R_DK ** -0.5)
    k = rope(k, pos).astype(jnp.float32)
    C = R_CHUNK if T % R_CHUNK == 0 else T
    n = T // C

    def to_chunks(a):
        return a.reshape(B, n, C, *a.shape[2:]).swapaxes(0, 1)

    log_g = retention_log_decay()

    def step(S, qkv):
        qc, kc, vc = qkv
        return retention_chunk(S, qc, kc, vc, log_g)

    S_new, o = lax.scan(step, S0.astype(jnp.float32), (to_chunks(q), to_chunks(k), to_chunks(v)))
    o = o.swapaxes(0, 1).reshape(B, T, R_HEADS, R_DV)
    o = o * lax.rsqrt(jnp.mean(o * o, axis=-1, keepdims=True) + EPS)
    o = o.reshape(B, T, hv).astype(x.dtype) * gn_g * jax.nn.silu(g)
    return x + gate[:, None, :] * (o @ w_out), S_new


def mla_shared_kv(h, c, pos, w_mod, b_mod, w_down, norm_g):
    shift, scale = adaln(c, w_mod, b_mod, 2)
    hn = modulated_norm(h, shift, scale)
    ckv = hn @ w_down
    lat = rmsnorm(ckv[..., :KV_LORA]) * norm_g
    kr = rope(ckv[..., KV_LORA:][:, :, None, :], pos)[:, :, 0, :]
    return lat, kr


def mla_queries(x, c, pos, w_mod, b_mod, w_in, q_norm_g, w_qb, w_uk):
    B, T, _ = x.shape
    shift, scale, gate = adaln(c, w_mod, b_mod, 3)
    xn = modulated_norm(x, shift, scale)
    proj = xn @ w_in
    qa = rmsnorm(proj[..., :Q_LORA]) * q_norm_g
    g = proj[..., Q_LORA:]
    q = (qa @ w_qb).reshape(B, T, M_HEADS, M_NOPE + M_ROPE)
    q_lat = jnp.einsum('bthn,chn->bthc', q[..., :M_NOPE], w_uk)
    q_rope = rope(q[..., M_NOPE:], pos)
    return q_lat, q_rope, g, gate


def mla_output(x, o_lat, g, gate, w_uv, w_o):
    B, T, _ = x.shape
    o = jnp.einsum('bthc,chv->bthv', o_lat, w_uv).reshape(B, T, M_HEADS * M_DV)
    return x + gate[:, None, :] * ((o * jax.nn.silu(g)) @ w_o)


def mla_attend_prompt(q_lat, q_rope, lat, kr):
    B, T, H, _ = q_lat.shape
    QB = Q_BLOCK if T % Q_BLOCK == 0 else T
    nb = T // QB

    def blocks(a):
        return a.reshape(B, nb, QB, *a.shape[2:]).swapaxes(0, 1)

    kpos = jnp.arange(T)

    def attend(args):
        ql, qr, start = args
        s = (jnp.einsum('bqhc,bkc->bhqk', ql, lat) + jnp.einsum('bqhr,bkr->bhqk', qr, kr)).astype(jnp.float32) * M_SCALE
        mask = kpos[None, :] <= (start + jnp.arange(QB))[:, None]
        p = jax.nn.softmax(jnp.where(mask, s, -jnp.inf), axis=-1).astype(lat.dtype)
        return jnp.einsum('bhqk,bkc->bqhc', p, lat)

    starts = jnp.arange(nb, dtype=jnp.int32) * QB
    o = lax.map(attend, (blocks(q_lat), blocks(q_rope), starts))
    return o.swapaxes(0, 1).reshape(B, T, H, KV_LORA)


def mla_attend_sample(q_lat, q_rope, lat_new, kr_new, cache_lat, cache_kr, page_table):
    B, T, H, _ = q_lat.shape
    lat_past = cache_lat[page_table].reshape(B, -1, KV_LORA)
    kr_past = cache_kr[page_table].reshape(B, -1, M_ROPE)
    P = lat_past.shape[1]
    s_past = jnp.einsum('bqhc,bkc->bhqk', q_lat, lat_past) + jnp.einsum('bqhr,bkr->bhqk', q_rope, kr_past)
    s_new = jnp.einsum('bqhc,bkc->bhqk', q_lat, lat_new) + jnp.einsum('bqhr,bkr->bhqk', q_rope, kr_new)
    causal = jnp.tril(jnp.ones((T, T), dtype=bool))
    s_new = jnp.where(causal, s_new.astype(jnp.float32) * M_SCALE, -jnp.inf)
    s = jnp.concatenate([s_past.astype(jnp.float32) * M_SCALE, s_new], axis=-1)
    p = jax.nn.softmax(s, axis=-1).astype(lat_new.dtype)
    return jnp.einsum('bhqk,bkc->bqhc', p[..., :P], lat_past) + jnp.einsum('bhqk,bkc->bqhc', p[..., P:], lat_new)


def setup_inputs(seed: int = 0) -> dict:
    key = jax.random.key(seed)
    ks = jax.random.split(key, 26)

    def nrm(k, shape, scale):
        return jax.random.normal(k, shape, jnp.float32) * scale

    D = D_MODEL
    n_pages = PAST_LEN // PAGE_SIZE
    n_used = DEC_BATCH * n_pages
    n_phys = n_used + max(1, n_used // 4)
    page_table = jax.random.permutation(ks[7], n_phys)[:n_used].reshape(DEC_BATCH, n_pages).astype(jnp.int32)
    r_in = 2 * R_HEADS * R_DK + 2 * R_HEADS * R_DV
    m_in = Q_LORA + M_HEADS * M_DV
    return {
        'x_prompt': nrm(ks[0], (BATCH, SEQ, D), 1.0),
        'x_sample': nrm(ks[1], (DEC_BATCH, DEC_SEQ, D), 1.0),
        'c_prompt': nrm(ks[2], (BATCH, D), 1.0),
        'c_sample': nrm(ks[3], (DEC_BATCH, D), 1.0),
        'state_retention': nrm(ks[4], (N_A_LAYERS, DEC_BATCH, R_HEADS, R_DK, R_DV), 1.0),
        'cache_kv_latent': nrm(ks[5], (n_phys, PAGE_SIZE, KV_LORA), 1.0),
        'cache_k_rope': nrm(ks[6], (n_phys, PAGE_SIZE, M_ROPE), 1.0),
        'page_table': page_table,
        'ret_w_mod': nrm(ks[8], (N_A_LAYERS, D, 3 * D), D ** -0.5),
        'ret_b_mod': nrm(ks[9], (N_A_LAYERS, 3 * D), 0.02),
        'ret_w_in': nrm(ks[10], (N_A_LAYERS, D, r_in), D ** -0.5),
        'ret_gn_g': 1.0 + nrm(ks[11], (N_A_LAYERS, R_HEADS * R_DV), 0.02),
        'ret_w_out': nrm(ks[12], (N_A_LAYERS, R_HEADS * R_DV, D), (R_HEADS * R_DV) ** -0.5),
        'kv_w_mod': nrm(ks[13], (D, 2 * D), D ** -0.5),
        'kv_b_mod': nrm(ks[14], (2 * D,), 0.02),
        'kv_w_down': nrm(ks[15], (D, KV_LORA + M_ROPE), D ** -0.5),
        'kv_norm_g': 1.0 + nrm(ks[16], (KV_LORA,), 0.02),
        'kv_w_uk': nrm(ks[17], (KV_LORA, M_HEADS, M_NOPE), KV_LORA ** -0.5),
        'kv_w_uv': nrm(ks[18], (KV_LORA, M_HEADS, M_DV), KV_LORA ** -0.5),
        'mla_w_mod': nrm(ks[19], (N_B_LAYERS, D, 3 * D), D ** -0.5),
        'mla_b_mod': nrm(ks[20], (N_B_LAYERS, 3 * D), 0.02),
        'mla_w_in': nrm(ks[21], (N_B_LAYERS, D, m_in), D ** -0.5),
        'mla_q_norm_g': 1.0 + nrm(ks[22], (N_B_LAYERS, Q_LORA), 0.02),
        'mla_w_qb': nrm(ks[23], (N_B_LAYERS, Q_LORA, M_HEADS * (M_NOPE + M_ROPE)), Q_LORA ** -0.5),
        'mla_w_o': nrm(ks[24], (N_B_LAYERS, M_HEADS * M_DV, D), (M_HEADS * M_DV) ** -0.5),
        'final_norm_g': 1.0 + nrm(ks[25], (D,), 0.02),
    }


def reference(x_prompt, x_sample, c_prompt, c_sample, state_retention, cache_kv_latent, cache_k_rope, page_table,
              ret_w_mod, ret_b_mod, ret_w_in, ret_gn_g, ret_w_out,
              kv_w_mod, kv_b_mod, kv_w_down, kv_norm_g, kv_w_uk, kv_w_uv,
              mla_w_mod, mla_b_mod, mla_w_in, mla_q_norm_g, mla_w_qb, mla_w_o, final_norm_g):
    past_len = page_table.shape[1] * cache_kv_latent.shape[1]
    pos_p = jnp.arange(x_prompt.shape[1], dtype=jnp.float32)
    pos_s = past_len + jnp.arange(x_sample.shape[1], dtype=jnp.float32)
    zero_state = jnp.zeros((x_prompt.shape[0], R_HEADS, R_DK, R_DV), jnp.float32)
    hp, hs = x_prompt, x_sample
    ret_p, ret_s = [], []
    for layer in range(DEPTH):
        if layer < N_A_LAYERS:
            a = layer
            hp, sp = retention_layer(hp, c_prompt, zero_state, pos_p, ret_w_mod[a], ret_b_mod[a], ret_w_in[a], ret_gn_g[a], ret_w_out[a])
            hs, ss = retention_layer(hs, c_sample, state_retention[a], pos_s, ret_w_mod[a], ret_b_mod[a], ret_w_in[a], ret_gn_g[a], ret_w_out[a])
            ret_p.append(sp)
            ret_s.append(ss.astype(state_retention.dtype))
        else:
            if layer == N_A_LAYERS:
                lat_p, kr_p = mla_shared_kv(hp, c_prompt, pos_p, kv_w_mod, kv_b_mod, kv_w_down, kv_norm_g)
                lat_s, kr_s = mla_shared_kv(hs, c_sample, pos_s, kv_w_mod, kv_b_mod, kv_w_down, kv_norm_g)
            b = layer - N_A_LAYERS
            ql, qr, g, gate = mla_queries(hp, c_prompt, pos_p, mla_w_mod[b], mla_b_mod[b], mla_w_in[b], mla_q_norm_g[b], mla_w_qb[b], kv_w_uk)
            hp = mla_output(hp, mla_attend_prompt(ql, qr, lat_p, kr_p), g, gate, kv_w_uv, mla_w_o[b])
            ql, qr, g, gate = mla_queries(hs, c_sample, pos_s, mla_w_mod[b], mla_b_mod[b], mla_w_in[b], mla_q_norm_g[b], mla_w_qb[b], kv_w_uk)
            hs = mla_output(hs, mla_attend_sample(ql, qr, lat_s, kr_s, cache_kv_latent, cache_k_rope, page_table), g, gate, kv_w_uv, mla_w_o[b])
    y_prompt = rmsnorm(hp) * final_norm_g
    y_sample = rmsnorm(hs) * final_norm_g
    return (y_prompt, y_sample, jnp.stack(ret_p), lat_p, kr_p, jnp.stack(ret_s), lat_s, kr_s)
```

```python
import functools
import math

import jax
import jax.numpy as jnp
from jax import lax
from jax.experimental import pallas as pl
from jax.experimental.pallas import tpu as pltpu

EPS = 1e-6
LOG_ROPE_BASE = math.log(10000.0)
R_CHUNK = 128
VMEM_LIMIT_BYTES = 56 * 1024 * 1024
LANES = 128
BF16_SUBLANES = 16

F32 = jnp.float32
BF16 = jnp.bfloat16
NT_DIMS = (((1,), (1,)), ((), ()))


def _cparams(n_axes):
    return pltpu.CompilerParams(dimension_semantics=("arbitrary",) * n_axes,
                                vmem_limit_bytes=VMEM_LIMIT_BYTES)


def _silu(x):
    return x * (1.0 / (1.0 + jnp.exp(-x)))


def _rms(x):
    return x * lax.rsqrt(jnp.mean(x * x, axis=-1, keepdims=True) + EPS)


def _adaln_kernel(c_ref, w_ref, b_ref, o_ref):
    a = _silu(c_ref[...]).astype(BF16)
    o_ref[...] = jnp.dot(a, w_ref[...].astype(BF16), preferred_element_type=F32) + b_ref[...]


def _adaln(c, w, b, tn=512):
    mc, d = c.shape
    n = w.shape[1]
    tn = min(tn, n)
    return pl.pallas_call(
        _adaln_kernel,
        out_shape=jax.ShapeDtypeStruct((mc, n), F32),
        grid=(n // tn,),
        in_specs=[pl.BlockSpec((mc, d), lambda j: (0, 0)),
                  pl.BlockSpec((d, tn), lambda j: (0, j)),
                  pl.BlockSpec((1, tn), lambda j: (0, j))],
        out_specs=pl.BlockSpec((mc, tn), lambda j: (0, j)),
        compiler_params=_cparams(1),
        name="adaln",
    )(c, w, b.reshape(1, n))


def _normmod_kernel(nsets, n_ptiles, x_ref, *refs):
    ins, outs = refs[:4 * nsets], refs[4 * nsets:]
    i = pl.program_id(0)
    xn = _rms(x_ref[...])
    for s in range(nsets):
        shp, scp, shs, scs = ins[4 * s:4 * s + 4]
        out = outs[s]

        @pl.when(i < n_ptiles)
        def _():
            out[...] = (xn * (1.0 + scp[0]) + shp[0]).astype(out.dtype)

        @pl.when(i >= n_ptiles)
        def _():
            out[...] = (xn * (1.0 + scs[...]) + shs[...]).astype(out.dtype)


def _normmod(x, mods, rows_per_seq, n_prompt_rows, tr):
    m, d = x.shape
    n_ptiles = n_prompt_rows // tr
    tiles_per_seq = rows_per_seq // tr
    n_seq = mods[0][0].shape[0]
    pspec = pl.BlockSpec((1, 1, d), lambda i: (jnp.minimum(i // tiles_per_seq, n_seq - 1), 0, 0))
    sspec = pl.BlockSpec((tr, d), lambda i: (jnp.maximum(i - n_ptiles, 0), 0))
    in_specs = [pl.BlockSpec((tr, d), lambda i: (i, 0))]
    args = [x]
    for (shp, scp, shs, scs) in mods:
        in_specs += [pspec, pspec, sspec, sspec]
        args += [shp, scp, shs, scs]
    outs = pl.pallas_call(
        functools.partial(_normmod_kernel, len(mods), n_ptiles),
        out_shape=[jax.ShapeDtypeStruct((m, d), BF16)] * len(mods),
        grid=(m // tr,),
        in_specs=in_specs,
        out_specs=[pl.BlockSpec((tr, d), lambda i: (i, 0))] * len(mods),
        compiler_params=_cparams(1),
        name="normmod",
    )(*args)
    return outs


def _rmsgain_kernel(x_ref, g_ref, o_ref):
    o_ref[...] = (_rms(x_ref[...]) * g_ref[...]).astype(o_ref.dtype)


def _rmsgain(x, g, width, out_dtype, tr):
    m = x.shape[0]
    return pl.pallas_call(
        _rmsgain_kernel,
        out_shape=jax.ShapeDtypeStruct((m, width), out_dtype),
        grid=(m // tr,),
        in_specs=[pl.BlockSpec((tr, width), lambda i: (i, 0)),
                  pl.BlockSpec((1, width), lambda i: (0, 0))],
        out_specs=pl.BlockSpec((tr, width), lambda i: (i, 0)),
        compiler_params=_cparams(1),
        name="rmsgain",
    )(x, g.reshape(1, width))


def _wdot(x_ref, w_ref, wb_ref):
    @pl.when(pl.program_id(1) == 0)
    def _():
        wb_ref[...] = w_ref[...].astype(BF16)

    return jnp.dot(x_ref[...], wb_ref[...], preferred_element_type=F32)


def _mm_plain_kernel(x_ref, w_ref, o_ref, wb_ref):
    o_ref[...] = _wdot(x_ref, w_ref, wb_ref).astype(o_ref.dtype)


def _mm_rope_pairs_kernel(n_q, n_rope, q_scale, half, x_ref, w_ref, cos_ref, sin_ref, o_ref, wb_ref):
    acc = _wdot(x_ref, w_ref, wb_ref)
    n = pl.program_id(0)
    tn = acc.shape[1]

    @pl.when(n < n_rope)
    def _():
        c, s = cos_ref[...], sin_ref[...]
        sc = jnp.where(n < n_q, q_scale, 1.0).astype(F32)
        for h in range(tn // (2 * half)):
            a, b = h * 2 * half, h * 2 * half + half
            x1, x2 = acc[:, a:b], acc[:, b:b + half]
            o_ref[:, a:b] = ((x1 * c - x2 * s) * sc).astype(o_ref.dtype)
            o_ref[:, b:b + half] = ((x2 * c + x1 * s) * sc).astype(o_ref.dtype)

    @pl.when(n >= n_rope)
    def _():
        o_ref[...] = acc.astype(o_ref.dtype)


def _mm_rope_lanes_kernel(n_plain, half, x_ref, w_ref, cos_ref, sin_ref, o_ref, wb_ref):
    acc = _wdot(x_ref, w_ref, wb_ref)
    n = pl.program_id(0)
    tm, tn = acc.shape

    @pl.when(n < n_plain)
    def _():
        o_ref[...] = acc.astype(o_ref.dtype)

    @pl.when(n >= n_plain)
    def _():
        c, s = cos_ref[...], sin_ref[...]
        lane = lax.broadcasted_iota(jnp.int32, (tm, LANES), 1)
        first = lax.rem(lane, 2 * half) < half
        for j in range(tn // LANES):
            x = acc[:, j * LANES:(j + 1) * LANES]
            rot = jnp.where(first, pltpu.roll(x, LANES - half, 1), pltpu.roll(x, half, 1))
            o_ref[:, j * LANES:(j + 1) * LANES] = (x * c + rot * s).astype(o_ref.dtype)


def _mm_residual_kernel(n_ptiles, x_ref, w_ref, res_ref, gp_ref, gs_ref, o_ref, wb_ref):
    acc = _wdot(x_ref, w_ref, wb_ref)
    m = pl.program_id(1)

    @pl.when(m < n_ptiles)
    def _():
        o_ref[...] = res_ref[...] + gp_ref[0] * acc

    @pl.when(m >= n_ptiles)
    def _():
        o_ref[...] = res_ref[...] + gs_ref[...] * acc


def _mm_call(kernel_fn, x, w, extra_args, extra_specs, out_dtype, tm, tn, name):
    m, k = x.shape
    n = w.shape[1]
    return pl.pallas_call(
        kernel_fn,
        out_shape=jax.ShapeDtypeStruct((m, n), out_dtype),
        grid=(n // tn, m // tm),
        in_specs=[pl.BlockSpec((tm, k), lambda j, i: (i, 0)),
                  pl.BlockSpec((k, tn), lambda j, i: (0, j))] + extra_specs,
        out_specs=pl.BlockSpec((tm, tn), lambda j, i: (i, j)),
        scratch_shapes=[pltpu.VMEM((k, tn), BF16)],
        compiler_params=_cparams(2),
        name=name,
    )(x, w, *extra_args)


def _mm_plain(x, w, out_dtype, tm, tn):
    return _mm_call(_mm_plain_kernel, x, w, [], [], out_dtype, tm, tn, "mm_plain")


def _mm_rope_pairs(x, w, cos, sin, n_q, n_rope, q_scale, out_dtype, tm, tn):
    half = cos.shape[1]
    tspec = pl.BlockSpec((tm, half), lambda j, i: (i, 0))
    return _mm_call(functools.partial(_mm_rope_pairs_kernel, n_q, n_rope, q_scale, half),
                    x, w, [cos, sin], [tspec, tspec], out_dtype, tm, tn, "mm_rope_pairs")


def _mm_rope_lanes(x, w, cos, sin, n_plain, half, out_dtype, tm, tn):
    tspec = pl.BlockSpec((tm, LANES), lambda j, i: (i, 0))
    return _mm_call(functools.partial(_mm_rope_lanes_kernel, n_plain, half),
                    x, w, [cos, sin], [tspec, tspec], out_dtype, tm, tn, "mm_rope_lanes")


def _mm_residual(x, w, res, gate_p, gate_s, rows_per_seq, n_prompt_rows, tm, tn):
    n_ptiles = n_prompt_rows // tm
    tiles_per_seq = rows_per_seq // tm
    n_seq = gate_p.shape[0]
    specs = [pl.BlockSpec((tm, tn), lambda j, i: (i, j)),
             pl.BlockSpec((1, 1, tn), lambda j, i: (jnp.minimum(i // tiles_per_seq, n_seq - 1), 0, j)),
             pl.BlockSpec((tm, tn), lambda j, i: (jnp.maximum(i - n_ptiles, 0), j))]
    return _mm_call(functools.partial(_mm_residual_kernel, n_ptiles),
                    x, w, [res, gate_p, gate_s], specs, F32, tm, tn, "mm_residual")


def _ret_prompt_kernel(n_chunks, lg_ref, gc_ref, q_ref, k_ref, v_ref, g_ref, gn_ref,
                       o_ref, sout_ref, s_ref):
    h, t = pl.program_id(1), pl.program_id(2)
    lg, gc = lg_ref[h], gc_ref[h]
    c = R_CHUNK

    @pl.when(t == 0)
    def _():
        s_ref[...] = jnp.zeros_like(s_ref)

    ri = lax.broadcasted_iota(jnp.int32, (c, c), 0)
    ci = lax.broadcasted_iota(jnp.int32, (c, c), 1)
    diff = (ri - ci).astype(F32)
    decay = jnp.where(diff >= 0, jnp.exp(lg * jnp.maximum(diff, 0.0)), 0.0)
    idx = lax.broadcasted_iota(jnp.int32, (c, 1), 0).astype(F32)
    q_decay = jnp.exp(lg * (idx + 1.0))
    k_decay = jnp.exp(lg * (c - 1.0 - idx))
    gn = gn_ref[...]

    for j in range(n_chunks):
        rows = slice(j * c, (j + 1) * c)
        q, k, v = q_ref[rows, :], k_ref[rows, :], v_ref[rows, :]
        state = s_ref[...]
        scores = lax.dot_general(q, k, NT_DIMS, preferred_element_type=F32) * decay
        o = (jnp.dot(scores.astype(BF16), v, preferred_element_type=F32)
             + jnp.dot(q, state.astype(BF16), preferred_element_type=F32) * q_decay)
        kd_t = (k.astype(F32) * k_decay).T.astype(BF16)
        s_ref[...] = gc * state + jnp.dot(kd_t, v, preferred_element_type=F32)
        o_ref[rows, :] = (_rms(o) * gn * _silu(g_ref[rows, :].astype(F32))).astype(o_ref.dtype)

    @pl.when(t == pl.num_programs(2) - 1)
    def _():
        sout_ref[0, 0] = s_ref[...]


def _ret_prompt(proj, gn_g, log_g, n_seq, seq_len, heads, dk, dv, tb):
    m = proj.shape[0]
    nt = seq_len // tb
    gc = jnp.exp(log_g * R_CHUNK)
    kcol, vcol, gcol = heads, (2 * heads * dk) // dv, (2 * heads * dk) // dv + heads

    def rowmap(col0):
        return lambda b, h, t, *_: (b * nt + t, col0 + h)

    grid_spec = pltpu.PrefetchScalarGridSpec(
        num_scalar_prefetch=2,
        grid=(n_seq, heads, nt),
        in_specs=[pl.BlockSpec((tb, dk), rowmap(0)),
                  pl.BlockSpec((tb, dk), rowmap(kcol)),
                  pl.BlockSpec((tb, dv), rowmap(vcol)),
                  pl.BlockSpec((tb, dv), rowmap(gcol)),
                  pl.BlockSpec((1, dv), lambda b, h, t, *_: (0, h))],
        out_specs=[pl.BlockSpec((tb, dv), rowmap(0)),
                   pl.BlockSpec((1, 1, dk, dv), lambda b, h, t, *_: (b, h, 0, 0))],
        scratch_shapes=[pltpu.VMEM((dk, dv), F32)],
    )
    return pl.pallas_call(
        functools.partial(_ret_prompt_kernel, tb // R_CHUNK),
        out_shape=[jax.ShapeDtypeStruct((m, heads * dv), BF16),
                   jax.ShapeDtypeStruct((n_seq, heads, dk, dv), F32)],
        grid_spec=grid_spec,
        compiler_params=_cparams(3),
        name="ret_prompt",
    )(log_g, gc, proj, proj, proj, proj, gn_g.reshape(1, heads * dv))


def _ret_sample_kernel(bb, t_len, lg_ref, gc_ref, q_ref, k_ref, v_ref, g_ref, gn_ref, s_ref, oin_ref,
                       o_ref, sout_ref):
    del oin_ref
    h = pl.program_id(1)
    lg, gc = lg_ref[h], gc_ref[h]
    r = bb * t_len
    ri = lax.broadcasted_iota(jnp.int32, (r, r), 0)
    ci = lax.broadcasted_iota(jnp.int32, (r, r), 1)
    diff = (ri - ci).astype(F32)
    same_seq = (ri // t_len) == (ci // t_len)
    decay = jnp.where(same_seq & (diff >= 0), jnp.exp(lg * jnp.maximum(diff, 0.0)), 0.0)
    pos = lax.rem(lax.broadcasted_iota(jnp.int32, (r, 1), 0), t_len).astype(F32)
    q_decay = jnp.exp(lg * (pos + 1.0))
    k_decay = jnp.exp(lg * (t_len - 1.0 - pos))
    row_seq = lax.broadcasted_iota(jnp.int32, (r, 1), 0) // t_len
    col_seq = lax.broadcasted_iota(jnp.int32, (1, r), 1) // t_len

    q, k, v = q_ref[...], k_ref[...], v_ref[...]
    scores = lax.dot_general(q, k, NT_DIMS, preferred_element_type=F32) * decay
    o = jnp.dot(scores.astype(BF16), v, preferred_element_type=F32)
    kd_t = (k.astype(F32) * k_decay).T
    cross = jnp.zeros_like(o)
    for b in range(bb):
        state = s_ref[b, 0]
        cross_b = jnp.dot(q, state.astype(BF16), preferred_element_type=F32)
        cross = jnp.where(row_seq == b, cross_b, cross)
        kd_b = jnp.where(col_seq == b, kd_t, 0.0).astype(BF16)
        sout_ref[b, 0] = gc * state + jnp.dot(kd_b, v, preferred_element_type=F32)
    o = o + cross * q_decay
    o_ref[...] = (_rms(o) * gn_ref[...] * _silu(g_ref[...].astype(F32))).astype(o_ref.dtype)


def _ret_sample(proj, o_all, state, gn_g, log_g, n_prompt_rows, t_len, heads, dk, dv, bb):
    n_seq = state.shape[0]
    r = bb * t_len
    row0 = n_prompt_rows // r
    gc = jnp.exp(log_g * t_len)
    kcol, vcol, gcol = heads, (2 * heads * dk) // dv, (2 * heads * dk) // dv + heads

    def rowmap(col0):
        return lambda i, h, *_: (row0 + i, col0 + h)

    sspec = pl.BlockSpec((bb, 1, dk, dv), lambda i, h, *_: (i, h, 0, 0))
    grid_spec = pltpu.PrefetchScalarGridSpec(
        num_scalar_prefetch=2,
        grid=(n_seq // bb, heads),
        in_specs=[pl.BlockSpec((r, dk), rowmap(0)),
                  pl.BlockSpec((r, dk), rowmap(kcol)),
                  pl.BlockSpec((r, dv), rowmap(vcol)),
                  pl.BlockSpec((r, dv), rowmap(gcol)),
                  pl.BlockSpec((1, dv), lambda i, h, *_: (0, h)),
                  sspec,
                  pl.BlockSpec(memory_space=pl.ANY)],
        out_specs=[pl.BlockSpec((r, dv), rowmap(0)), sspec],
    )
    return pl.pallas_call(
        functools.partial(_ret_sample_kernel, bb, t_len),
        out_shape=[jax.ShapeDtypeStruct(o_all.shape, o_all.dtype),
                   jax.ShapeDtypeStruct(state.shape, F32)],
        grid_spec=grid_spec,
        input_output_aliases={8: 0},
        compiler_params=_cparams(2),
        name="ret_sample",
    )(log_g, gc, proj, proj, proj, proj, gn_g.reshape(1, heads * dv), state, o_all)


def _kv_down_kernel(x_ref, wl_ref, wr_ref, ng_ref, cos_ref, sin_ref,
                    lat_ref, kr_ref, latb_ref, krb_ref, wlb_ref, wrb_ref):
    @pl.when(pl.program_id(0) == 0)
    def _():
        wlb_ref[...] = wl_ref[...].astype(BF16)
        wrb_ref[...] = wr_ref[...].astype(BF16)

    x = x_ref[...]
    lat = _rms(jnp.dot(x, wlb_ref[...], preferred_element_type=F32)) * ng_ref[...]
    lat_ref[...] = lat
    latb_ref[...] = lat.astype(BF16)
    c = jnp.dot(x, wrb_ref[...], preferred_element_type=F32)
    half = c.shape[1] // 2
    rot = jnp.concatenate([c[:, half:], c[:, :half]], axis=-1)
    kr = c * cos_ref[...] + rot * sin_ref[...]
    kr_ref[...] = kr
    krb_ref[...] = kr.astype(BF16)


def _kv_down(xn, w_down, norm_g, cos, sin, lora, tm):
    m, d = xn.shape
    rope = w_down.shape[1] - lora
    w_lat, w_rope = w_down[:, :lora], w_down[:, lora:]
    row = lambda i: (i, 0)
    fixed = lambda i: (0, 0)
    return pl.pallas_call(
        _kv_down_kernel,
        out_shape=[jax.ShapeDtypeStruct((m, lora), F32), jax.ShapeDtypeStruct((m, rope), F32),
                   jax.ShapeDtypeStruct((m, lora), BF16), jax.ShapeDtypeStruct((m, rope), BF16)],
        grid=(m // tm,),
        in_specs=[pl.BlockSpec((tm, d), row), pl.BlockSpec((d, lora), fixed), pl.BlockSpec((d, rope), fixed),
                  pl.BlockSpec((1, lora), fixed), pl.BlockSpec((tm, rope), row), pl.BlockSpec((tm, rope), row)],
        out_specs=[pl.BlockSpec((tm, lora), row), pl.BlockSpec((tm, rope), row),
                   pl.BlockSpec((tm, lora), row), pl.BlockSpec((tm, rope), row)],
        scratch_shapes=[pltpu.VMEM((d, lora), BF16), pltpu.VMEM((d, rope), BF16)],
        compiler_params=_cparams(1),
        name="kv_down",
    )(xn, w_lat, w_rope, norm_g.reshape(1, lora), cos, sin)


def _head_in_kernel(x_ref, w_ref, o_ref):
    o_ref[0] = jnp.dot(x_ref[...], w_ref[0].astype(BF16), preferred_element_type=F32).astype(o_ref.dtype)


def _head_in(x, w, heads, tm):
    m = x.shape[0]
    _, kin, nout = w.shape
    return pl.pallas_call(
        _head_in_kernel,
        out_shape=jax.ShapeDtypeStruct((heads, m, nout), BF16),
        grid=(heads, m // tm),
        in_specs=[pl.BlockSpec((tm, kin), lambda h, i: (i, h)),
                  pl.BlockSpec((1, kin, nout), lambda h, i: (h, 0, 0))],
        out_specs=pl.BlockSpec((1, tm, nout), lambda h, i: (h, i, 0)),
        compiler_params=_cparams(2),
        name="head_in",
    )(x, w)


def _head_out_kernel(x_ref, w_ref, g_ref, o_ref):
    o = jnp.dot(x_ref[0], w_ref[0].astype(BF16), preferred_element_type=F32)
    o_ref[...] = (o * _silu(g_ref[...])).astype(o_ref.dtype)


def _head_out(x, w, g, g_col0, tm):
    heads, m, kin = x.shape
    nout = w.shape[2]
    gblk = g_col0 // nout
    return pl.pallas_call(
        _head_out_kernel,
        out_shape=jax.ShapeDtypeStruct((m, heads * nout), BF16),
        grid=(heads, m // tm),
        in_specs=[pl.BlockSpec((1, tm, kin), lambda h, i: (h, i, 0)),
                  pl.BlockSpec((1, kin, nout), lambda h, i: (h, 0, 0)),
                  pl.BlockSpec((tm, nout), lambda h, i: (i, gblk + h))],
        out_specs=pl.BlockSpec((tm, nout), lambda h, i: (i, h)),
        compiler_params=_cparams(2),
        name="head_out",
    )(x, w, g)


def _softmax_update(s, v, m_ref, l_ref, acc_ref):
    m_prev = m_ref[...]
    m_new = jnp.maximum(m_prev, jnp.max(s, axis=-1, keepdims=True))
    p = jnp.exp(s - m_new)
    alpha = jnp.exp(m_prev - m_new)
    l_ref[...] = alpha * l_ref[...] + jnp.sum(p, axis=-1, keepdims=True)
    acc_ref[...] = alpha * acc_ref[...] + jnp.dot(p.astype(BF16), v, preferred_element_type=F32)
    m_ref[...] = m_new


def _softmax_init(m_ref, l_ref, acc_ref):
    m_ref[...] = jnp.full_like(m_ref, -jnp.inf)
    l_ref[...] = jnp.zeros_like(l_ref)
    acc_ref[...] = jnp.zeros_like(acc_ref)


def _attn_prompt_kernel(scale, ql_ref, qr_ref, k_ref, kr_ref, o_ref, m_ref, l_ref, acc_ref):
    qi, ki = pl.program_id(1), pl.program_id(3)
    g, tq, lora = ql_ref.shape
    tk = k_ref.shape[0]

    @pl.when(ki == 0)
    def _():
        _softmax_init(m_ref, l_ref, acc_ref)

    def scores():
        q = ql_ref[...].reshape(g * tq, lora)
        qr = qr_ref[...].reshape(g * tq, qr_ref.shape[2])
        return (lax.dot_general(q, k_ref[...], NT_DIMS, preferred_element_type=F32)
                + lax.dot_general(qr, kr_ref[...], NT_DIMS, preferred_element_type=F32)) * scale

    @pl.when(ki < qi)
    def _():
        _softmax_update(scores(), k_ref[...], m_ref, l_ref, acc_ref)

    @pl.when(ki == qi)
    def _():
        tok = lax.rem(lax.broadcasted_iota(jnp.int32, (g * tq, tk), 0), tq)
        key = lax.broadcasted_iota(jnp.int32, (g * tq, tk), 1)
        s = jnp.where(key <= tok, scores(), -jnp.inf)
        _softmax_update(s, k_ref[...], m_ref, l_ref, acc_ref)

    @pl.when(ki == pl.num_programs(3) - 1)
    def _():
        o_ref[...] = (acc_ref[...] * (1.0 / l_ref[...])).reshape(g, tq, lora).astype(o_ref.dtype)


def _attn_prompt(q_lat, q_rope, lat, kr, n_seq, seq_len, scale, g, tq):
    heads, m, lora = q_lat.shape
    rope = q_rope.shape[2]
    nq = seq_len // tq
    qmap = lambda b, qi, hg, ki: (hg, b * nq + qi, 0)
    kmap = lambda b, qi, hg, ki: (b * nq + jnp.minimum(ki, qi), 0)
    return pl.pallas_call(
        functools.partial(_attn_prompt_kernel, scale),
        out_shape=jax.ShapeDtypeStruct((heads, m, lora), BF16),
        grid=(n_seq, nq, heads // g, nq),
        in_specs=[pl.BlockSpec((g, tq, lora), qmap), pl.BlockSpec((g, tq, rope), qmap),
                  pl.BlockSpec((tq, lora), kmap), pl.BlockSpec((tq, rope), kmap)],
        out_specs=pl.BlockSpec((g, tq, lora), qmap),
        scratch_shapes=[pltpu.VMEM((g * tq, 1), F32), pltpu.VMEM((g * tq, 1), F32),
                        pltpu.VMEM((g * tq, lora), F32)],
        compiler_params=_cparams(4),
        name="attn_prompt",
    )(q_lat, q_rope, lat, kr)


def _attn_sample_kernel(pg, t_len, scale, pt_ref, ql_ref, qr_ref, kn_ref, krn_ref, *refs):
    del pt_ref
    lat_refs, kr_refs = refs[:pg], refs[pg:2 * pg]
    o_ref, kb_ref, krb_ref, m_ref, l_ref, acc_ref = refs[2 * pg:]
    j = pl.program_id(1)
    page = lat_refs[0].shape[1]

    @pl.when(j == 0)
    def _():
        _softmax_init(m_ref, l_ref, acc_ref)

    for i in range(pg):
        kb_ref[i * page:(i + 1) * page, :] = lat_refs[i][0].astype(BF16)
        krb_ref[i * page:(i + 1) * page, :] = kr_refs[i][0].astype(BF16)

    q, qr = ql_ref[0], qr_ref[0]

    def scores(k, kr):
        return (lax.dot_general(q, k, NT_DIMS, preferred_element_type=F32)
                + lax.dot_general(qr, kr, NT_DIMS, preferred_element_type=F32)) * scale

    _softmax_update(scores(kb_ref[...], krb_ref[...]), kb_ref[...], m_ref, l_ref, acc_ref)

    @pl.when(j == pl.num_programs(1) - 1)
    def _():
        kn, krn = kn_ref[0], krn_ref[0]
        rows, nk = q.shape[0], kn.shape[0]
        tok = lax.rem(lax.broadcasted_iota(jnp.int32, (rows, nk), 0), t_len)
        key = lax.broadcasted_iota(jnp.int32, (rows, nk), 1)
        s = jnp.where(key <= tok, scores(kn, krn), -jnp.inf)
        _softmax_update(s, kn, m_ref, l_ref, acc_ref)
        o_ref[0] = (acc_ref[...] * (1.0 / l_ref[...])).astype(o_ref.dtype)


def _attn_sample(q_lat, q_rope, lat_new, kr_new, cache_lat, cache_kr, page_table, t_len, scale, pg):
    n_seq, rows, lora = q_lat.shape
    rope = q_rope.shape[2]
    page = cache_lat.shape[1]
    n_pages = page_table.shape[1]
    nk = lat_new.shape[1]
    steps = n_pages // pg
    seqmap = lambda b, j, pt: (b, 0, 0)

    def pagemap(i):
        return lambda b, j, pt: (pt[b * n_pages + j * pg + i], 0, 0)

    grid_spec = pltpu.PrefetchScalarGridSpec(
        num_scalar_prefetch=1,
        grid=(n_seq, steps),
        in_specs=([pl.BlockSpec((1, rows, lora), seqmap), pl.BlockSpec((1, rows, rope), seqmap),
                   pl.BlockSpec((1, nk, lora), seqmap), pl.BlockSpec((1, nk, rope), seqmap)]
                  + [pl.BlockSpec((1, page, lora), pagemap(i)) for i in range(pg)]
                  + [pl.BlockSpec((1, page, rope), pagemap(i)) for i in range(pg)]),
        out_specs=pl.BlockSpec((1, rows, lora), seqmap),
        scratch_shapes=[pltpu.VMEM((pg * page, lora), BF16), pltpu.VMEM((pg * page, rope), BF16),
                        pltpu.VMEM((rows, 1), F32), pltpu.VMEM((rows, 1), F32),
                        pltpu.VMEM((rows, lora), F32)],
    )
    return pl.pallas_call(
        functools.partial(_attn_sample_kernel, pg, t_len, scale),
        out_shape=jax.ShapeDtypeStruct((n_seq, rows, lora), BF16),
        grid_spec=grid_spec,
        compiler_params=_cparams(2),
        name="attn_sample",
    )(page_table.reshape(-1), q_lat, q_rope, lat_new, kr_new,
      *([cache_lat] * pg), *([cache_kr] * pg))


def _rope_angles(pos, half):
    inv = jnp.exp(-LOG_ROPE_BASE * jnp.arange(half, dtype=F32) / half)
    ang = pos[:, None] * inv[None, :]
    return jnp.cos(ang), jnp.sin(ang)


def _col_tile(*widths, pref=512):
    t = pref
    while any(w % t for w in widths):
        t //= 2
    assert t % LANES == 0
    return t


def _split_mod(mod, n, n_prompt_seq, n_sample_seq, t_sample):
    d = mod.shape[1] // n
    out = []
    for i in range(n):
        v = mod[:, i * d:(i + 1) * d]
        vp = v[:n_prompt_seq].reshape(n_prompt_seq, 1, d)
        vs = jnp.repeat(v[n_prompt_seq:n_prompt_seq + n_sample_seq], t_sample, axis=0)
        out.append((vp, vs))
    return out


def kernel(x_prompt, x_sample, c_prompt, c_sample, state_retention, cache_kv_latent, cache_k_rope, page_table, ret_w_mod, ret_b_mod, ret_w_in, ret_gn_g, ret_w_out, kv_w_mod, kv_b_mod, kv_w_down, kv_norm_g, kv_w_uk, kv_w_uv, mla_w_mod, mla_b_mod, mla_w_in, mla_q_norm_g, mla_w_qb, mla_w_o, final_norm_g):
    bp, t_p, d = x_prompt.shape
    bs, t_s, _ = x_sample.shape
    n_ret, _, r_heads, r_dk, r_dv = state_retention.shape
    n_mla = mla_w_mod.shape[0]
    lora, m_heads, nope = kv_w_uk.shape
    m_dv = kv_w_uv.shape[2]
    rope = cache_k_rope.shape[2]
    q_lora = mla_q_norm_g.shape[1]
    page = cache_kv_latent.shape[1]
    past_len = page_table.shape[1] * page
    mp, ms = bp * t_p, bs * t_s
    m = mp + ms

    tm = ms
    assert t_p % tm == 0 and tm % BF16_SUBLANES == 0
    tr = min(256, tm)
    tn = 512
    tb = min(512, t_p)
    assert t_p % tb == 0 and tb % R_CHUNK == 0
    bb = 8
    assert bs % bb == 0 and (bb * t_s) % BF16_SUBLANES == 0 and mp % (bb * t_s) == 0
    assert r_dk == 2 * LANES and r_dv == 2 * r_dk and tn % r_dv == 0
    assert rope * 2 == LANES and nope == LANES and m_dv == LANES

    pos = jnp.concatenate([jnp.tile(jnp.arange(t_p, dtype=F32), bp),
                           jnp.tile(past_len + jnp.arange(t_s, dtype=F32), bs)])
    cos_r, sin_r = _rope_angles(pos, r_dk // 2)
    cos_m, sin_m = _rope_angles(pos, rope // 2)
    cos_kr = jnp.concatenate([cos_m, cos_m], axis=1)
    sin_kr = jnp.concatenate([-sin_m, sin_m], axis=1)
    cos_q = jnp.concatenate([cos_kr, cos_kr], axis=1)
    sin_q = jnp.concatenate([sin_kr, sin_kr], axis=1)

    log_g = jnp.log1p(-jnp.exp2(-5.0 - jnp.arange(r_heads, dtype=F32)))

    n_cond = bp + bs
    c_all = jnp.concatenate([c_prompt, c_sample], axis=0)
    c_all = jnp.pad(c_all, ((0, (-n_cond) % BF16_SUBLANES), (0, 0)))
    h = jnp.concatenate([x_prompt.reshape(mp, d), x_sample.reshape(ms, d)], axis=0)

    split = functools.partial(_split_mod, n_prompt_seq=bp, n_sample_seq=bs, t_sample=t_s)
    ret_p, ret_s = [], []
    for a in range(n_ret):
        (sh_p, sh_s), (sc_p, sc_s), (gt_p, gt_s) = split(_adaln(c_all, ret_w_mod[a], ret_b_mod[a]), 3)
        (xn,) = _normmod(h, [(sh_p, sc_p, sh_s, sc_s)], t_p, mp, tr)
        hk = r_heads * r_dk
        proj = _mm_rope_pairs(xn, ret_w_in[a], cos_r, sin_r, hk // tn, 2 * hk // tn, r_dk ** -0.5, BF16, tm, tn)
        o_all, s_p = _ret_prompt(proj, ret_gn_g[a], log_g, bp, t_p, r_heads, r_dk, r_dv, tb)
        o_all, s_s = _ret_sample(proj, o_all, state_retention[a], ret_gn_g[a], log_g, mp, t_s,
                                 r_heads, r_dk, r_dv, bb)
        h = _mm_residual(o_all, ret_w_out[a], h, gt_p, gt_s, t_p, mp, tm, 256)
        ret_p.append(s_p)
        ret_s.append(s_s)

    lat = kr = None
    for b in range(n_mla):
        (sh_p, sh_s), (sc_p, sc_s), (gt_p, gt_s) = split(_adaln(c_all, mla_w_mod[b], mla_b_mod[b]), 3)
        if b == 0:
            (ksh_p, ksh_s), (ksc_p, ksc_s) = split(_adaln(c_all, kv_w_mod, kv_b_mod), 2)
            xn, xn_kv = _normmod(h, [(sh_p, sc_p, sh_s, sc_s), (ksh_p, ksc_p, ksh_s, ksc_s)], t_p, mp, tr)
            lat, kr, lat_b, kr_b = _kv_down(xn_kv, kv_w_down, kv_norm_g, cos_kr, sin_kr, lora, tm)
            pad = ((0, 0), (0, BF16_SUBLANES - t_s), (0, 0))
            lat_new = jnp.pad(lat_b[mp:].reshape(bs, t_s, lora), pad)
            kr_new = jnp.pad(kr_b[mp:].reshape(bs, t_s, rope), pad)
            w_uk_h = jnp.transpose(kv_w_uk, (1, 2, 0))
            w_uv_h = jnp.transpose(kv_w_uv, (1, 0, 2))
        else:
            (xn,) = _normmod(h, [(sh_p, sc_p, sh_s, sc_s)], t_p, mp, tr)
        proj = _mm_plain(xn, mla_w_in[b], F32, tm, _col_tile(mla_w_in.shape[2]))
        qa = _rmsgain(proj, mla_q_norm_g[b], q_lora, BF16, tr)
        w_qb = mla_w_qb[b].reshape(q_lora, m_heads, nope + rope)
        w_qb = jnp.concatenate([w_qb[:, :, :nope].reshape(q_lora, m_heads * nope),
                                w_qb[:, :, nope:].reshape(q_lora, m_heads * rope)], axis=1)
        tq_n = _col_tile(m_heads * nope, m_heads * rope)
        q = _mm_rope_lanes(qa, w_qb, cos_q, sin_q, m_heads * nope // tq_n, rope // 2, BF16, tm, tq_n)
        q_lat = _head_in(q, w_uk_h, m_heads, m // 4)
        q_rope = jnp.transpose(q[:, m_heads * nope:].reshape(m, m_heads, rope), (1, 0, 2))
        scale = 1.0 / math.sqrt(nope + rope)
        o_lat = _attn_prompt(q_lat, q_rope, lat_b, kr_b, bp, t_p, scale, min(8, m_heads), min(256, t_p))

        def to_seq(a):
            return jnp.transpose(a[:, mp:].reshape(m_heads, bs, t_s, -1), (1, 0, 2, 3)).reshape(bs, m_heads * t_s, -1)

        o_s = _attn_sample(to_seq(q_lat), to_seq(q_rope), lat_new, kr_new, cache_kv_latent, cache_k_rope,
                           page_table, t_s, scale, min(16, page_table.shape[1]))
        o_s = jnp.transpose(o_s.reshape(bs, m_heads, t_s, lora), (1, 0, 2, 3)).reshape(m_heads, ms, lora)
        o_lat = lax.dynamic_update_slice(o_lat, o_s, (0, mp, 0))
        o = _head_out(o_lat, w_uv_h, proj, q_lora, m // 4)
        h = _mm_residual(o, mla_w_o[b], h, gt_p, gt_s, t_p, mp, tm, _col_tile(d))

    y = _rmsgain(h, final_norm_g, d, F32, tr)
    return (y[:mp].reshape(bp, t_p, d), y[mp:].reshape(bs, t_s, d),
            jnp.stack(ret_p),
            lat[:mp].reshape(bp, t_p, lora), kr[:mp].reshape(bp, t_p, rope),
            jnp.stack(ret_s).astype(state_retention.dtype),
            lat[mp:].reshape(bs, t_s, lora), kr[mp:].reshape(bs, t_s, rope))
```

```python
import functools
import math

import jax
import jax.numpy as jnp
from jax import lax
from jax.experimental import pallas as pl
from jax.experimental.pallas import tpu as pltpu

EPS = 1e-6
LOG_ROPE_BASE = math.log(10000.0)
R_CHUNK = 128
VMEM_LIMIT_BYTES = 56 * 1024 * 1024
LANES = 128
BF16_SUBLANES = 16

F32 = jnp.float32
BF16 = jnp.bfloat16
NT_DIMS = (((1,), (1,)), ((), ()))


def _cparams(n_axes):
    return pltpu.CompilerParams(dimension_semantics=("arbitrary",) * n_axes,
                                vmem_limit_bytes=VMEM_LIMIT_BYTES)


def _silu(x):
    return x * (1.0 / (1.0 + jnp.exp(-x)))


def _rms(x):
    return x * lax.rsqrt(jnp.mean(x * x, axis=-1, keepdims=True) + EPS)


def _adaln_kernel(c_ref, w_ref, b_ref, o_ref):
    a = _silu(c_ref[...]).astype(BF16)
    o_ref[...] = jnp.dot(a, w_ref[...].astype(BF16), preferred_element_type=F32) + b_ref[...]


def _adaln(c, w, b, tn=512):
    mc, d = c.shape
    n = w.shape[1]
    tn = min(tn, n)
    return pl.pallas_call(
        _adaln_kernel,
        out_shape=jax.ShapeDtypeStruct((mc, n), F32),
        grid=(n // tn,),
        in_specs=[pl.BlockSpec((mc, d), lambda j: (0, 0)),
                  pl.BlockSpec((d, tn), lambda j: (0, j)),
                  pl.BlockSpec((1, tn), lambda j: (0, j))],
        out_specs=pl.BlockSpec((mc, tn), lambda j: (0, j)),
        compiler_params=_cparams(1),
        name="adaln",
    )(c, w, b.reshape(1, n))


def _normmod_kernel(nsets, n_ptiles, x_ref, *refs):
    ins, outs = refs[:4 * nsets], refs[4 * nsets:]
    i = pl.program_id(0)
    xn = _rms(x_ref[...])
    for s in range(nsets):
        shp, scp, shs, scs = ins[4 * s:4 * s + 4]
        out = outs[s]

        @pl.when(i < n_ptiles)
        def _():
            out[...] = (xn * (1.0 + scp[0]) + shp[0]).astype(out.dtype)

        @pl.when(i >= n_ptiles)
        def _():
            out[...] = (xn * (1.0 + scs[...]) + shs[...]).astype(out.dtype)


def _normmod(x, mods, rows_per_seq, n_prompt_rows, tr):
    m, d = x.shape
    n_ptiles = n_prompt_rows // tr
    tiles_per_seq = rows_per_seq // tr
    n_seq = mods[0][0].shape[0]
    pspec = pl.BlockSpec((1, 1, d), lambda i: (jnp.minimum(i // tiles_per_seq, n_seq - 1), 0, 0))
    sspec = pl.BlockSpec((tr, d), lambda i: (jnp.maximum(i - n_ptiles, 0), 0))
    in_specs = [pl.BlockSpec((tr, d), lambda i: (i, 0))]
    args = [x]
    for (shp, scp, shs, scs) in mods:
        in_specs += [pspec, pspec, sspec, sspec]
        args += [shp, scp, shs, scs]
    outs = pl.pallas_call(
        functools.partial(_normmod_kernel, len(mods), n_ptiles),
        out_shape=[jax.ShapeDtypeStruct((m, d), BF16)] * len(mods),
        grid=(m // tr,),
        in_specs=in_specs,
        out_specs=[pl.BlockSpec((tr, d), lambda i: (i, 0))] * len(mods),
        compiler_params=_cparams(1),
        name="normmod",
    )(*args)
    return outs


def _rmsgain_kernel(x_ref, g_ref, o_ref):
    o_ref[...] = (_rms(x_ref[...]) * g_ref[...]).astype(o_ref.dtype)


def _rmsgain(x, g, width, out_dtype, tr):
    m = x.shape[0]
    return pl.pallas_call(
        _rmsgain_kernel,
        out_shape=jax.ShapeDtypeStruct((m, width), out_dtype),
        grid=(m // tr,),
        in_specs=[pl.BlockSpec((tr, width), lambda i: (i, 0)),
                  pl.BlockSpec((1, width), lambda i: (0, 0))],
        out_specs=pl.BlockSpec((tr, width), lambda i: (i, 0)),
        compiler_params=_cparams(1),
        name="rmsgain",
    )(x, g.reshape(1, width))


def _rmsgain_split_kernel(n_ptiles, x_ref, g_ref, op_ref, os_ref):
    y = _rms(x_ref[...]) * g_ref[...]
    i = pl.program_id(0)

    @pl.when(i < n_ptiles)
    def _():
        op_ref[...] = y

    @pl.when(i >= n_ptiles)
    def _():
        os_ref[...] = y


def _rmsgain_split(x, g, n_prompt_rows, tr):
    m, d = x.shape
    n_ptiles = n_prompt_rows // tr
    return pl.pallas_call(
        functools.partial(_rmsgain_split_kernel, n_ptiles),
        out_shape=[jax.ShapeDtypeStruct((n_prompt_rows, d), F32),
                   jax.ShapeDtypeStruct((m - n_prompt_rows, d), F32)],
        grid=(m // tr,),
        in_specs=[pl.BlockSpec((tr, d), lambda i: (i, 0)),
                  pl.BlockSpec((1, d), lambda i: (0, 0))],
        out_specs=[pl.BlockSpec((tr, d), lambda i: (jnp.minimum(i, n_ptiles - 1), 0)),
                   pl.BlockSpec((tr, d), lambda i: (jnp.maximum(i - n_ptiles, 0), 0))],
        compiler_params=_cparams(1),
        name="rmsgain_split",
    )(x, g.reshape(1, d))


def _wdot(x_ref, w_ref, wb_ref):
    @pl.when(pl.program_id(1) == 0)
    def _():
        wb_ref[...] = w_ref[...].astype(BF16)

    return jnp.dot(x_ref[...], wb_ref[...], preferred_element_type=F32)


def _mm_plain_kernel(x_ref, w_ref, o_ref, wb_ref):
    o_ref[...] = _wdot(x_ref, w_ref, wb_ref).astype(o_ref.dtype)


def _mm_rope_pairs_kernel(n_q, n_rope, q_scale, half, x_ref, w_ref, cos_ref, sin_ref, o_ref, wb_ref):
    acc = _wdot(x_ref, w_ref, wb_ref)
    n = pl.program_id(0)
    tn = acc.shape[1]
    sc = jnp.where(n < n_q, q_scale, 1.0).astype(F32)
    c = jnp.where(n < n_rope, cos_ref[...], 1.0) * sc
    s = jnp.where(n < n_rope, sin_ref[...], 0.0) * sc
    for h in range(tn // (2 * half)):
        a, b = h * 2 * half, h * 2 * half + half
        x1, x2 = acc[:, a:b], acc[:, b:b + half]
        o_ref[:, a:b] = (x1 * c - x2 * s).astype(o_ref.dtype)
        o_ref[:, b:b + half] = (x2 * c + x1 * s).astype(o_ref.dtype)


def _mm_rope_lanes_kernel(n_plain, half, x_ref, w_ref, cos_ref, sin_ref, o_ref, wb_ref):
    acc = _wdot(x_ref, w_ref, wb_ref)
    n = pl.program_id(0)
    tm, tn = acc.shape

    c = jnp.where(n >= n_plain, cos_ref[...], 1.0)
    s = jnp.where(n >= n_plain, sin_ref[...], 0.0)
    lane = lax.broadcasted_iota(jnp.int32, (tm, LANES), 1)
    first = lax.rem(lane, 2 * half) < half
    for j in range(tn // LANES):
        x = acc[:, j * LANES:(j + 1) * LANES]
        rot = jnp.where(first, pltpu.roll(x, LANES - half, 1), pltpu.roll(x, half, 1))
        o_ref[:, j * LANES:(j + 1) * LANES] = (x * c + rot * s).astype(o_ref.dtype)


def _mm_residual_kernel(n_ptiles, x_ref, w_ref, res_ref, gp_ref, gs_ref, o_ref, wb_ref):
    acc = _wdot(x_ref, w_ref, wb_ref)
    gate = jnp.where(pl.program_id(1) < n_ptiles, gp_ref[0], gs_ref[...])
    o_ref[...] = res_ref[...] + gate * acc


def _mm_call(kernel_fn, x, w, extra_args, extra_specs, out_dtype, tm, tn, name):
    m, k = x.shape
    n = w.shape[1]
    single = 2 * k * tn * w.dtype.itemsize > VMEM_LIMIT_BYTES // 2
    w_spec = pl.BlockSpec((k, tn), lambda j, i: (0, j), **({"pipeline_mode": pl.Buffered(1)} if single else {}))
    return pl.pallas_call(
        kernel_fn,
        out_shape=jax.ShapeDtypeStruct((m, n), out_dtype),
        grid=(n // tn, m // tm),
        in_specs=[pl.BlockSpec((tm, k), lambda j, i: (i, 0)), w_spec] + extra_specs,
        out_specs=pl.BlockSpec((tm, tn), lambda j, i: (i, j)),
        scratch_shapes=[pltpu.VMEM((k, tn), BF16)],
        compiler_params=_cparams(2),
        name=name,
    )(x, w, *extra_args)


def _mm_plain(x, w, out_dtype, tm, tn):
    return _mm_call(_mm_plain_kernel, x, w, [], [], out_dtype, tm, tn, "mm_plain")


def _mm_rope_pairs(x, w, cos, sin, n_q, n_rope, q_scale, out_dtype, tm, tn):
    half = cos.shape[1]
    tspec = pl.BlockSpec((tm, half), lambda j, i: (i, 0))
    return _mm_call(functools.partial(_mm_rope_pairs_kernel, n_q, n_rope, q_scale, half),
                    x, w, [cos, sin], [tspec, tspec], out_dtype, tm, tn, "mm_rope_pairs")


def _mm_rope_lanes(x, w, cos, sin, n_plain, half, out_dtype, tm, tn):
    tspec = pl.BlockSpec((tm, LANES), lambda j, i: (i, 0))
    return _mm_call(functools.partial(_mm_rope_lanes_kernel, n_plain, half),
                    x, w, [cos, sin], [tspec, tspec], out_dtype, tm, tn, "mm_rope_lanes")


def _mm_residual(x, w, res, gate_p, gate_s, rows_per_seq, n_prompt_rows, tm, tn):
    n_ptiles = n_prompt_rows // tm
    tiles_per_seq = rows_per_seq // tm
    n_seq = gate_p.shape[0]
    specs = [pl.BlockSpec((tm, tn), lambda j, i: (i, j)),
             pl.BlockSpec((1, 1, tn), lambda j, i: (jnp.minimum(i // tiles_per_seq, n_seq - 1), 0, j)),
             pl.BlockSpec((tm, tn), lambda j, i: (jnp.maximum(i - n_ptiles, 0), j))]
    return _mm_call(functools.partial(_mm_residual_kernel, n_ptiles),
                    x, w, [res, gate_p, gate_s], specs, F32, tm, tn, "mm_residual")


def _ret_prompt_kernel(n_chunks, lg_ref, gc_ref, q_ref, k_ref, v_ref, g_ref, gn_ref,
                       o_ref, sout_ref, s_ref):
    h, t = pl.program_id(1), pl.program_id(2)
    lg, gc = lg_ref[h], gc_ref[h]
    c = R_CHUNK

    @pl.when(t == 0)
    def _():
        s_ref[...] = jnp.zeros_like(s_ref)

    ri = lax.broadcasted_iota(jnp.int32, (c, c), 0)
    ci = lax.broadcasted_iota(jnp.int32, (c, c), 1)
    diff = (ri - ci).astype(F32)
    decay = jnp.where(diff >= 0, jnp.exp(lg * jnp.maximum(diff, 0.0)), 0.0)
    idx = lax.broadcasted_iota(jnp.int32, (c, 1), 0).astype(F32)
    q_decay = jnp.exp(lg * (idx + 1.0))
    k_decay = jnp.exp(lg * (c - 1.0 - idx))
    gn = gn_ref[...]

    for j in range(n_chunks):
        rows = slice(j * c, (j + 1) * c)
        q, k, v = q_ref[rows, :], k_ref[rows, :], v_ref[rows, :]
        state = s_ref[...]
        scores = lax.dot_general(q, k, NT_DIMS, preferred_element_type=F32) * decay
        o = (jnp.dot(scores.astype(BF16), v, preferred_element_type=F32)
             + jnp.dot(q, state.astype(BF16), preferred_element_type=F32) * q_decay)
        kd_t = (k.astype(F32) * k_decay).T.astype(BF16)
        s_ref[...] = gc * state + jnp.dot(kd_t, v, preferred_element_type=F32)
        o_ref[rows, :] = (_rms(o) * gn * _silu(g_ref[rows, :].astype(F32))).astype(o_ref.dtype)

    @pl.when(t == pl.num_programs(2) - 1)
    def _():
        sout_ref[0, 0] = s_ref[...]


def _ret_prompt(proj, gn_g, log_g, n_seq, seq_len, heads, dk, dv, tb):
    m = proj.shape[0]
    nt = seq_len // tb
    gc = jnp.exp(log_g * R_CHUNK)
    kcol, vcol, gcol = heads, (2 * heads * dk) // dv, (2 * heads * dk) // dv + heads

    def rowmap(col0):
        return lambda b, h, t, *_: (b * nt + t, col0 + h)

    grid_spec = pltpu.PrefetchScalarGridSpec(
        num_scalar_prefetch=2,
        grid=(n_seq, heads, nt),
        in_specs=[pl.BlockSpec((tb, dk), rowmap(0)),
                  pl.BlockSpec((tb, dk), rowmap(kcol)),
                  pl.BlockSpec((tb, dv), rowmap(vcol)),
                  pl.BlockSpec((tb, dv), rowmap(gcol)),
                  pl.BlockSpec((1, dv), lambda b, h, t, *_: (0, h))],
        out_specs=[pl.BlockSpec((tb, dv), rowmap(0)),
                   pl.BlockSpec((1, 1, dk, dv), lambda b, h, t, *_: (b, h, 0, 0))],
        scratch_shapes=[pltpu.VMEM((dk, dv), F32)],
    )
    return pl.pallas_call(
        functools.partial(_ret_prompt_kernel, tb // R_CHUNK),
        out_shape=[jax.ShapeDtypeStruct((m, heads * dv), BF16),
                   jax.ShapeDtypeStruct((n_seq, heads, dk, dv), F32)],
        grid_spec=grid_spec,
        compiler_params=_cparams(3),
        name="ret_prompt",
    )(log_g, gc, proj, proj, proj, proj, gn_g.reshape(1, heads * dv))


def _ret_sample_kernel(bb, t_len, lg_ref, gc_ref, q_ref, k_ref, v_ref, g_ref, gn_ref, s_ref, *refs):
    o_ref, sout_ref = refs[-2:]
    h = pl.program_id(1)
    lg, gc = lg_ref[h], gc_ref[h]
    r = bb * t_len
    ri = lax.broadcasted_iota(jnp.int32, (r, r), 0)
    ci = lax.broadcasted_iota(jnp.int32, (r, r), 1)
    diff = (ri - ci).astype(F32)
    same_seq = (ri // t_len) == (ci // t_len)
    decay = jnp.where(same_seq & (diff >= 0), jnp.exp(lg * jnp.maximum(diff, 0.0)), 0.0)
    pos = lax.rem(lax.broadcasted_iota(jnp.int32, (r, 1), 0), t_len).astype(F32)
    q_decay = jnp.exp(lg * (pos + 1.0))
    k_decay = jnp.exp(lg * (t_len - 1.0 - pos))
    row_seq = lax.broadcasted_iota(jnp.int32, (r, 1), 0) // t_len
    col_seq = lax.broadcasted_iota(jnp.int32, (1, r), 1) // t_len

    q, k, v = q_ref[...], k_ref[...], v_ref[...]
    scores = lax.dot_general(q, k, NT_DIMS, preferred_element_type=F32) * decay
    o = jnp.dot(scores.astype(BF16), v, preferred_element_type=F32)
    kd_t = (k.astype(F32) * k_decay).T
    cross = jnp.zeros_like(o)
    for b in range(bb):
        state = s_ref[b, 0]
        cross_b = jnp.dot(q, state.astype(BF16), preferred_element_type=F32)
        cross = jnp.where(row_seq == b, cross_b, cross)
        kd_b = jnp.where(col_seq == b, kd_t, 0.0).astype(BF16)
        sout_ref[b, 0] = (gc * state + jnp.dot(kd_b, v, preferred_element_type=F32)).astype(sout_ref.dtype)
    o = o + cross * q_decay
    o_ref[...] = (_rms(o) * gn_ref[...] * _silu(g_ref[...].astype(F32))).astype(o_ref.dtype)


def _ret_sample(proj, o_all, states, new_states, layer, gn_g, log_g, n_prompt_rows, t_len, heads, dk, dv, bb):
    n_seq = states.shape[1]
    r = bb * t_len
    row0 = n_prompt_rows // r
    gc = jnp.exp(log_g * t_len)
    kcol, vcol, gcol = heads, (2 * heads * dk) // dv, (2 * heads * dk) // dv + heads

    def rowmap(col0):
        return lambda i, h, *_: (row0 + i, col0 + h)

    sspec = pl.BlockSpec((None, bb, 1, dk, dv), lambda i, h, *_: (layer, i, h, 0, 0))
    any_spec = pl.BlockSpec(memory_space=pl.ANY)
    args = [log_g, gc, proj, proj, proj, proj, gn_g.reshape(1, heads * dv), states, o_all]
    aliases = {8: 0}
    if new_states is not None:
        args.append(new_states)
        aliases[9] = 1
    grid_spec = pltpu.PrefetchScalarGridSpec(
        num_scalar_prefetch=2,
        grid=(n_seq // bb, heads),
        in_specs=[pl.BlockSpec((r, dk), rowmap(0)),
                  pl.BlockSpec((r, dk), rowmap(kcol)),
                  pl.BlockSpec((r, dv), rowmap(vcol)),
                  pl.BlockSpec((r, dv), rowmap(gcol)),
                  pl.BlockSpec((1, dv), lambda i, h, *_: (0, h)),
                  sspec] + [any_spec] * (len(args) - 8),
        out_specs=[pl.BlockSpec((r, dv), rowmap(0)), sspec],
    )
    return pl.pallas_call(
        functools.partial(_ret_sample_kernel, bb, t_len),
        out_shape=[jax.ShapeDtypeStruct(o_all.shape, o_all.dtype),
                   jax.ShapeDtypeStruct(states.shape, states.dtype)],
        grid_spec=grid_spec,
        input_output_aliases=aliases,
        compiler_params=_cparams(2),
        name="ret_sample",
    )(*args)


def _kv_down_kernel(n_ptiles, x_ref, wl_ref, wr_ref, ng_ref, cos_ref, sin_ref,
                    latp_ref, krp_ref, lats_ref, krs_ref, latb_ref, krb_ref, wlb_ref, wrb_ref):
    i = pl.program_id(0)

    @pl.when(i == 0)
    def _():
        wlb_ref[...] = wl_ref[...].astype(BF16)
        wrb_ref[...] = wr_ref[...].astype(BF16)

    x = x_ref[...]
    lat = _rms(jnp.dot(x, wlb_ref[...], preferred_element_type=F32)) * ng_ref[...]
    c = jnp.dot(x, wrb_ref[...], preferred_element_type=F32)
    half = c.shape[1] // 2
    rot = jnp.concatenate([c[:, half:], c[:, :half]], axis=-1)
    kr = c * cos_ref[...] + rot * sin_ref[...]
    latb_ref[...] = lat.astype(BF16)
    krb_ref[...] = kr.astype(BF16)

    @pl.when(i < n_ptiles)
    def _():
        latp_ref[...] = lat
        krp_ref[...] = kr

    @pl.when(i >= n_ptiles)
    def _():
        lats_ref[...] = lat
        krs_ref[...] = kr


def _kv_down(xn, w_down, norm_g, cos, sin, lora, n_prompt_rows, tm):
    m, d = xn.shape
    rope = w_down.shape[1] - lora
    n_ptiles = n_prompt_rows // tm
    w_lat, w_rope = w_down[:, :lora], w_down[:, lora:]
    row = lambda i: (i, 0)
    fixed = lambda i: (0, 0)
    prow = lambda i: (jnp.minimum(i, n_ptiles - 1), 0)
    srow = lambda i: (jnp.maximum(i - n_ptiles, 0), 0)
    ms = m - n_prompt_rows
    return pl.pallas_call(
        functools.partial(_kv_down_kernel, n_ptiles),
        out_shape=[jax.ShapeDtypeStruct((n_prompt_rows, lora), F32), jax.ShapeDtypeStruct((n_prompt_rows, rope), F32),
                   jax.ShapeDtypeStruct((ms, lora), F32), jax.ShapeDtypeStruct((ms, rope), F32),
                   jax.ShapeDtypeStruct((m, lora), BF16), jax.ShapeDtypeStruct((m, rope), BF16)],
        grid=(m // tm,),
        in_specs=[pl.BlockSpec((tm, d), row), pl.BlockSpec((d, lora), fixed), pl.BlockSpec((d, rope), fixed),
                  pl.BlockSpec((1, lora), fixed), pl.BlockSpec((tm, rope), row), pl.BlockSpec((tm, rope), row)],
        out_specs=[pl.BlockSpec((tm, lora), prow), pl.BlockSpec((tm, rope), prow),
                   pl.BlockSpec((tm, lora), srow), pl.BlockSpec((tm, rope), srow),
                   pl.BlockSpec((tm, lora), row), pl.BlockSpec((tm, rope), row)],
        scratch_shapes=[pltpu.VMEM((d, lora), BF16), pltpu.VMEM((d, rope), BF16)],
        compiler_params=_cparams(1),
        name="kv_down",
    )(xn, w_lat, w_rope, norm_g.reshape(1, lora), cos, sin)


def _head_in_kernel(x_ref, w_ref, o_ref):
    o_ref[0] = jnp.dot(x_ref[...], w_ref[0].astype(BF16), preferred_element_type=F32).astype(o_ref.dtype)


def _head_in(x, w, row0, rows):
    heads, kin, nout = w.shape
    rb = row0 // rows
    return pl.pallas_call(
        _head_in_kernel,
        out_shape=jax.ShapeDtypeStruct((heads, rows, nout), BF16),
        grid=(heads,),
        in_specs=[pl.BlockSpec((rows, kin), lambda h: (rb, h)),
                  pl.BlockSpec((1, kin, nout), lambda h: (h, 0, 0))],
        out_specs=pl.BlockSpec((1, rows, nout), lambda h: (h, 0, 0)),
        compiler_params=_cparams(1),
        name="head_in",
    )(x, w)


def _head_out_kernel(x_ref, w_ref, g_ref, oin_ref, o_ref):
    del oin_ref
    o = jnp.dot(x_ref[0], w_ref[0].astype(BF16), preferred_element_type=F32)
    o_ref[...] = (o * _silu(g_ref[...])).astype(o_ref.dtype)


def _head_out(x, w, g, g_col0, o_all, row0):
    heads, rows, kin = x.shape
    nout = w.shape[2]
    gblk = g_col0 // nout
    rb = row0 // rows
    return pl.pallas_call(
        _head_out_kernel,
        out_shape=jax.ShapeDtypeStruct(o_all.shape, o_all.dtype),
        grid=(heads,),
        in_specs=[pl.BlockSpec((1, rows, kin), lambda h: (h, 0, 0)),
                  pl.BlockSpec((1, kin, nout), lambda h: (h, 0, 0)),
                  pl.BlockSpec((rows, nout), lambda h: (rb, gblk + h)),
                  pl.BlockSpec(memory_space=pl.ANY)],
        out_specs=pl.BlockSpec((rows, nout), lambda h: (rb, h)),
        input_output_aliases={3: 0},
        compiler_params=_cparams(1),
        name="head_out",
    )(x, w, g, o_all)


def _lanes(x, n):
    return jnp.tile(x, (1, n // LANES)) if n >= LANES else x[:, :n]


def _softmax_update(s, v, c, m_ref, l_ref, acc_ref):
    m_prev = m_ref[...]
    m_new = jnp.maximum(m_prev, jnp.max(s, axis=-1, keepdims=True))
    p = jnp.exp2((s - _lanes(m_new, s.shape[1])) * c)
    alpha = jnp.exp2((m_prev - m_new) * c)
    l_ref[...] = alpha * l_ref[...] + jnp.sum(p, axis=-1, keepdims=True)
    acc_ref[...] = (_lanes(alpha, acc_ref.shape[1]) * acc_ref[...]
                    + jnp.dot(p.astype(BF16), v, preferred_element_type=F32))
    m_ref[...] = m_new


def _softmax_init(m_ref, l_ref, acc_ref):
    m_ref[...] = jnp.full_like(m_ref, -jnp.inf)
    l_ref[...] = jnp.zeros_like(l_ref)
    acc_ref[...] = jnp.zeros_like(acc_ref)


def _attn_prompt_kernel(g, c, qi_ref, ki_ref, qn_ref, qr_ref, wuk_ref, k_ref, kr_ref, wuv_ref, gate_ref,
                        o_ref, ql_s, qr_s, m_s, l_s, acc_s):
    t = pl.program_id(2)
    qi, ki = qi_ref[t], ki_ref[t]
    tq = qn_ref.shape[0]
    nope, rope, dv = qn_ref.shape[1] // g, qr_ref.shape[1] // g, o_ref.shape[1] // g
    tk = k_ref.shape[0]

    @pl.when(ki == 0)
    def _():
        for h in range(g):
            rows = pl.ds(h * tq, tq)
            ql_s[rows, :] = jnp.dot(qn_ref[:, h * nope:(h + 1) * nope], wuk_ref[h].astype(BF16),
                                    preferred_element_type=F32).astype(BF16)
            qr_s[rows, :] = qr_ref[:, h * rope:(h + 1) * rope]
        _softmax_init(m_s, l_s, acc_s)

    def update(masked):
        k, kr = k_ref[...], kr_ref[...]
        if masked:
            visible = (lax.broadcasted_iota(jnp.int32, (tq, tk), 1)
                       <= lax.broadcasted_iota(jnp.int32, (tq, tk), 0))
        for h in range(g):
            rows = pl.ds(h * tq, tq)
            s = (lax.dot_general(ql_s[rows, :], k, NT_DIMS, preferred_element_type=F32)
                 + lax.dot_general(qr_s[rows, :], kr, NT_DIMS, preferred_element_type=F32))
            if masked:
                s = jnp.where(visible, s, -jnp.inf)
            _softmax_update(s, k, c, m_s.at[rows, :], l_s.at[rows, :], acc_s.at[rows, :])

    @pl.when(ki < qi)
    def _():
        update(False)

    @pl.when(ki == qi)
    def _():
        update(True)
        for h in range(g):
            rows = pl.ds(h * tq, tq)
            o_lat = acc_s[rows, :] * _lanes(1.0 / l_s[rows, :], acc_s.shape[1])
            o = jnp.dot(o_lat.astype(BF16), wuv_ref[h].astype(BF16), preferred_element_type=F32)
            cols = slice(h * dv, (h + 1) * dv)
            o_ref[:, cols] = (o * _silu(gate_ref[:, cols])).astype(o_ref.dtype)


def _attn_prompt(q, w_uk_h, lat, kr, w_uv_h, gate, gate_col0, n_seq, seq_len, scale, g, tq):
    m = q.shape[0]
    heads, nope, lora = w_uk_h.shape
    dv = w_uv_h.shape[2]
    rope = kr.shape[1]
    nq = seq_len // tq
    pairs = [(i, j) for i in range(nq) for j in range(i + 1)]
    qi_tab = jnp.asarray([p[0] for p in pairs], jnp.int32)
    ki_tab = jnp.asarray([p[1] for p in pairs], jnp.int32)
    rope_blk0 = heads * nope // (g * rope)
    gate_blk0 = gate_col0 // (g * dv)
    assert heads * nope % (g * rope) == 0 and gate_col0 % (g * dv) == 0

    def qrow(col0):
        return lambda b, hg, t, qi, ki: (b * nq + qi[t], col0 + hg)

    krow = lambda b, hg, t, qi, ki: (b * nq + ki[t], 0)
    hmap = lambda b, hg, t, qi, ki: (hg, 0, 0)
    grid_spec = pltpu.PrefetchScalarGridSpec(
        num_scalar_prefetch=2,
        grid=(n_seq, heads // g, len(pairs)),
        in_specs=[pl.BlockSpec((tq, g * nope), qrow(0)),
                  pl.BlockSpec((tq, g * rope), qrow(rope_blk0)),
                  pl.BlockSpec((g, nope, lora), hmap),
                  pl.BlockSpec((tq, lora), krow),
                  pl.BlockSpec((tq, rope), krow),
                  pl.BlockSpec((g, lora, dv), hmap),
                  pl.BlockSpec((tq, g * dv), qrow(gate_blk0))],
        out_specs=pl.BlockSpec((tq, g * dv), qrow(0)),
        scratch_shapes=[pltpu.VMEM((g * tq, lora), BF16), pltpu.VMEM((g * tq, rope), BF16),
                        pltpu.VMEM((g * tq, LANES), F32), pltpu.VMEM((g * tq, LANES), F32),
                        pltpu.VMEM((g * tq, lora), F32)],
    )
    return pl.pallas_call(
        functools.partial(_attn_prompt_kernel, g, scale * math.log2(math.e)),
        out_shape=jax.ShapeDtypeStruct((m, heads * dv), BF16),
        grid_spec=grid_spec,
        compiler_params=_cparams(3),
        name="attn_prompt",
    )(qi_tab, ki_tab, q, q, w_uk_h, lat, kr, w_uv_h, gate)


def _attn_sample_kernel(ns, pg, t_len, c, pt_ref, ql_ref, qr_ref, kn_ref, krn_ref, *refs):
    del pt_ref
    lat_refs, kr_refs = refs[:ns * pg], refs[ns * pg:2 * ns * pg]
    o_ref, kb_s, krb_s, m_s, l_s, acc_s = refs[2 * ns * pg:]
    j = pl.program_id(1)
    page = lat_refs[0].shape[1]

    @pl.when(j == 0)
    def _():
        _softmax_init(m_s, l_s, acc_s)

    for b in range(ns):
        for i in range(pg):
            kb_s[b, i * page:(i + 1) * page, :] = lat_refs[b * pg + i][0].astype(BF16)
            krb_s[b, i * page:(i + 1) * page, :] = kr_refs[b * pg + i][0].astype(BF16)
        k, kr = kb_s[b], krb_s[b]
        s = (lax.dot_general(ql_ref[b], k, NT_DIMS, preferred_element_type=F32)
             + lax.dot_general(qr_ref[b], kr, NT_DIMS, preferred_element_type=F32))
        _softmax_update(s, k, c, m_s.at[b], l_s.at[b], acc_s.at[b])

    @pl.when(j == pl.num_programs(1) - 1)
    def _():
        rows, nk = ql_ref.shape[1], kn_ref.shape[1]
        tok = lax.rem(lax.broadcasted_iota(jnp.int32, (rows, nk), 0), t_len)
        visible = lax.broadcasted_iota(jnp.int32, (rows, nk), 1) <= tok
        for b in range(ns):
            kn, krn = kn_ref[b], krn_ref[b]
            s = (lax.dot_general(ql_ref[b], kn, NT_DIMS, preferred_element_type=F32)
                 + lax.dot_general(qr_ref[b], krn, NT_DIMS, preferred_element_type=F32))
            _softmax_update(jnp.where(visible, s, -jnp.inf), kn, c, m_s.at[b], l_s.at[b], acc_s.at[b])
            o_ref[b] = (acc_s[b] * _lanes(1.0 / l_s[b], acc_s.shape[2])).astype(o_ref.dtype)


def _attn_sample(q_lat, q_rope, lat_new, kr_new, cache_lat, cache_kr, page_table, t_len, scale, ns, pg):
    n_seq, rows, lora = q_lat.shape
    rope = q_rope.shape[2]
    page = cache_lat.shape[1]
    n_pages = page_table.shape[1]
    nk = lat_new.shape[1]
    seqmap = lambda i, j, pt: (i, 0, 0)

    def pagemap(b, p):
        return lambda i, j, pt: (pt[(i * ns + b) * n_pages + j * pg + p], 0, 0)

    slots = [(b, p) for b in range(ns) for p in range(pg)]
    grid_spec = pltpu.PrefetchScalarGridSpec(
        num_scalar_prefetch=1,
        grid=(n_seq // ns, n_pages // pg),
        in_specs=([pl.BlockSpec((ns, rows, lora), seqmap), pl.BlockSpec((ns, rows, rope), seqmap),
                   pl.BlockSpec((ns, nk, lora), seqmap), pl.BlockSpec((ns, nk, rope), seqmap)]
                  + [pl.BlockSpec((1, page, lora), pagemap(b, p)) for b, p in slots]
                  + [pl.BlockSpec((1, page, rope), pagemap(b, p)) for b, p in slots]),
        out_specs=pl.BlockSpec((ns, rows, lora), seqmap),
        scratch_shapes=[pltpu.VMEM((ns, pg * page, lora), BF16), pltpu.VMEM((ns, pg * page, rope), BF16),
                        pltpu.VMEM((ns, rows, LANES), F32), pltpu.VMEM((ns, rows, LANES), F32),
                        pltpu.VMEM((ns, rows, lora), F32)],
    )
    return pl.pallas_call(
        functools.partial(_attn_sample_kernel, ns, pg, t_len, scale * math.log2(math.e)),
        out_shape=jax.ShapeDtypeStruct((n_seq, rows, lora), BF16),
        grid_spec=grid_spec,
        compiler_params=_cparams(2),
        name="attn_sample",
    )(page_table.reshape(-1), q_lat, q_rope, lat_new, kr_new,
      *([cache_lat] * len(slots)), *([cache_kr] * len(slots)))


def _rope_angles(pos, half):
    inv = jnp.exp(-LOG_ROPE_BASE * jnp.arange(half, dtype=F32) / half)
    ang = pos[:, None] * inv[None, :]
    return jnp.cos(ang), jnp.sin(ang)


def _col_tile(*widths, pref=512):
    t = pref
    while any(w % t for w in widths):
        t //= 2
    assert t % LANES == 0
    return t


def _split_mod(mod, n, n_prompt_seq, n_sample_seq, t_sample):
    d = mod.shape[1] // n
    out = []
    for i in range(n):
        v = mod[:, i * d:(i + 1) * d]
        vp = v[:n_prompt_seq].reshape(n_prompt_seq, 1, d)
        vs = jnp.repeat(v[n_prompt_seq:n_prompt_seq + n_sample_seq], t_sample, axis=0)
        out.append((vp, vs))
    return out


def kernel(x_prompt, x_sample, c_prompt, c_sample, state_retention, cache_kv_latent, cache_k_rope, page_table, ret_w_mod, ret_b_mod, ret_w_in, ret_gn_g, ret_w_out, kv_w_mod, kv_b_mod, kv_w_down, kv_norm_g, kv_w_uk, kv_w_uv, mla_w_mod, mla_b_mod, mla_w_in, mla_q_norm_g, mla_w_qb, mla_w_o, final_norm_g):
    bp, t_p, d = x_prompt.shape
    bs, t_s, _ = x_sample.shape
    n_ret, _, r_heads, r_dk, r_dv = state_retention.shape
    n_mla = mla_w_mod.shape[0]
    lora, m_heads, nope = kv_w_uk.shape
    m_dv = kv_w_uv.shape[2]
    rope = cache_k_rope.shape[2]
    q_lora = mla_q_norm_g.shape[1]
    page = cache_kv_latent.shape[1]
    past_len = page_table.shape[1] * page
    mp, ms = bp * t_p, bs * t_s
    m = mp + ms

    tm = ms
    assert t_p % tm == 0 and tm % BF16_SUBLANES == 0
    tr = min(256, tm)
    tn = 512
    tb = min(512, t_p)
    assert t_p % tb == 0 and tb % R_CHUNK == 0
    bb = 8
    assert bs % bb == 0 and (bb * t_s) % BF16_SUBLANES == 0 and mp % (bb * t_s) == 0
    assert r_dk == 2 * LANES and r_dv == 2 * r_dk and tn % r_dv == 0
    assert rope * 2 == LANES and nope == LANES and m_dv == LANES

    pos = jnp.concatenate([jnp.tile(jnp.arange(t_p, dtype=F32), bp),
                           jnp.tile(past_len + jnp.arange(t_s, dtype=F32), bs)])
    cos_r, sin_r = _rope_angles(pos, r_dk // 2)
    cos_m, sin_m = _rope_angles(pos, rope // 2)
    cos_kr = jnp.concatenate([cos_m, cos_m], axis=1)
    sin_kr = jnp.concatenate([-sin_m, sin_m], axis=1)
    cos_q = jnp.concatenate([cos_kr, cos_kr], axis=1)
    sin_q = jnp.concatenate([sin_kr, sin_kr], axis=1)

    log_g = jnp.log1p(-jnp.exp2(-5.0 - jnp.arange(r_heads, dtype=F32)))

    n_cond = bp + bs
    c_all = jnp.concatenate([c_prompt, c_sample], axis=0)
    c_all = jnp.pad(c_all, ((0, (-n_cond) % BF16_SUBLANES), (0, 0)))
    h = jnp.concatenate([x_prompt.reshape(mp, d), x_sample.reshape(ms, d)], axis=0)

    split = functools.partial(_split_mod, n_prompt_seq=bp, n_sample_seq=bs, t_sample=t_s)
    ret_p, ret_s = [], None
    for a in range(n_ret):
        (sh_p, sh_s), (sc_p, sc_s), (gt_p, gt_s) = split(_adaln(c_all, ret_w_mod[a], ret_b_mod[a]), 3)
        (xn,) = _normmod(h, [(sh_p, sc_p, sh_s, sc_s)], t_p, mp, tr)
        hk = r_heads * r_dk
        proj = _mm_rope_pairs(xn, ret_w_in[a], cos_r, sin_r, hk // tn, 2 * hk // tn, r_dk ** -0.5, BF16, tm, tn)
        o_all, s_p = _ret_prompt(proj, ret_gn_g[a], log_g, bp, t_p, r_heads, r_dk, r_dv, tb)
        o_all, ret_s = _ret_sample(proj, o_all, state_retention, ret_s, a, ret_gn_g[a], log_g, mp, t_s,
                                   r_heads, r_dk, r_dv, bb)
        h = _mm_residual(o_all, ret_w_out[a], h, gt_p, gt_s, t_p, mp, tm, _col_tile(d))
        ret_p.append(s_p)

    lat_p = kr_p = lat_s = kr_s = None
    for b in range(n_mla):
        (sh_p, sh_s), (sc_p, sc_s), (gt_p, gt_s) = split(_adaln(c_all, mla_w_mod[b], mla_b_mod[b]), 3)
        if b == 0:
            (ksh_p, ksh_s), (ksc_p, ksc_s) = split(_adaln(c_all, kv_w_mod, kv_b_mod), 2)
            xn, xn_kv = _normmod(h, [(sh_p, sc_p, sh_s, sc_s), (ksh_p, ksc_p, ksh_s, ksc_s)], t_p, mp, tr)
            lat_p, kr_p, lat_s, kr_s, lat_b, kr_b = _kv_down(xn_kv, kv_w_down, kv_norm_g, cos_kr, sin_kr,
                                                             lora, mp, tm)
            pad = ((0, 0), (0, BF16_SUBLANES - t_s), (0, 0))
            lat_new = jnp.pad(lat_b[mp:].reshape(bs, t_s, lora), pad)
            kr_new = jnp.pad(kr_b[mp:].reshape(bs, t_s, rope), pad)
            w_uk_h = jnp.transpose(kv_w_uk, (1, 2, 0))
            w_uv_h = jnp.transpose(kv_w_uv, (1, 0, 2))
        else:
            (xn,) = _normmod(h, [(sh_p, sc_p, sh_s, sc_s)], t_p, mp, tr)
        proj = _mm_plain(xn, mla_w_in[b], F32, tm, _col_tile(mla_w_in.shape[2]))
        qa = _rmsgain(proj, mla_q_norm_g[b], q_lora, BF16, tr)
        w_qb = mla_w_qb[b].reshape(q_lora, m_heads, nope + rope)
        w_qb = jnp.concatenate([w_qb[:, :, :nope].reshape(q_lora, m_heads * nope),
                                w_qb[:, :, nope:].reshape(q_lora, m_heads * rope)], axis=1)
        tq_n = _col_tile(m_heads * nope, m_heads * rope)
        q = _mm_rope_lanes(qa, w_qb, cos_q, sin_q, m_heads * nope // tq_n, rope // 2, BF16, tm, tq_n)
        scale = 1.0 / math.sqrt(nope + rope)
        o = _attn_prompt(q, w_uk_h, lat_b, kr_b, w_uv_h, proj, q_lora, bp, t_p, scale,
                         min(4, m_heads), min(512, t_p))

        q_lat_s = _head_in(q, w_uk_h, mp, ms)
        q_lat_s = jnp.transpose(q_lat_s.reshape(m_heads, bs, t_s, lora), (1, 0, 2, 3)).reshape(bs, m_heads * t_s, lora)
        q_rope_s = jnp.transpose(q[mp:, m_heads * nope:].reshape(bs, t_s, m_heads, rope),
                                 (0, 2, 1, 3)).reshape(bs, m_heads * t_s, rope)
        o_s = _attn_sample(q_lat_s, q_rope_s, lat_new, kr_new, cache_kv_latent, cache_k_rope,
                           page_table, t_s, scale, 2, min(16, page_table.shape[1]))
        o_s = jnp.transpose(o_s.reshape(bs, m_heads, t_s, lora), (1, 0, 2, 3)).reshape(m_heads, ms, lora)
        o = _head_out(o_s, w_uv_h, proj, q_lora, o, mp)
        h = _mm_residual(o, mla_w_o[b], h, gt_p, gt_s, t_p, mp, tm, _col_tile(d))

    y_p, y_s = _rmsgain_split(h, final_norm_g, mp, tr)
    return (y_p.reshape(bp, t_p, d), y_s.reshape(bs, t_s, d),
            jnp.stack(ret_p),
            lat_p.reshape(bp, t_p, lora), kr_p.reshape(bp, t_p, rope),
            ret_s,
            lat_s.reshape(bs, t_s, lora), kr_s.reshape(bs, t_s, rope))
```

```python
import functools
import math

import jax
import jax.numpy as jnp
from jax import lax
from jax.experimental import pallas as pl
from jax.experimental.pallas import tpu as pltpu

EPS = 1e-6
LOG_ROPE_BASE = math.log(10000.0)
R_CHUNK = 128
VMEM_LIMIT_BYTES = 56 * 1024 * 1024
LANES = 128
BF16_SUBLANES = 16

F32 = jnp.float32
BF16 = jnp.bfloat16
NT_DIMS = (((1,), (1,)), ((), ()))


def _cparams(n_axes):
    return pltpu.CompilerParams(dimension_semantics=("arbitrary",) * n_axes,
                                vmem_limit_bytes=VMEM_LIMIT_BYTES)


def _silu(x):
    return x * (1.0 / (1.0 + jnp.exp(-x)))


def _rms(x):
    return x * lax.rsqrt(jnp.mean(x * x, axis=-1, keepdims=True) + EPS)


def _adaln_kernel(c_ref, w_ref, b_ref, o_ref):
    a = _silu(c_ref[...]).astype(BF16)
    o_ref[...] = jnp.dot(a, w_ref[...].astype(BF16), preferred_element_type=F32) + b_ref[...]


def _adaln(c, w, b, tn=512):
    mc, d = c.shape
    n = w.shape[1]
    tn = min(tn, n)
    return pl.pallas_call(
        _adaln_kernel,
        out_shape=jax.ShapeDtypeStruct((mc, n), F32),
        grid=(n // tn,),
        in_specs=[pl.BlockSpec((mc, d), lambda j: (0, 0)),
                  pl.BlockSpec((d, tn), lambda j: (0, j)),
                  pl.BlockSpec((1, tn), lambda j: (0, j))],
        out_specs=pl.BlockSpec((mc, tn), lambda j: (0, j)),
        compiler_params=_cparams(1),
        name="adaln",
    )(c, w, b.reshape(1, n))


def _normmod_kernel(nsets, n_ptiles, x_ref, *refs):
    ins, outs = refs[:4 * nsets], refs[4 * nsets:]
    i = pl.program_id(0)
    xn = _rms(x_ref[...])
    for s in range(nsets):
        shp, scp, shs, scs = ins[4 * s:4 * s + 4]
        out = outs[s]

        @pl.when(i < n_ptiles)
        def _():
            out[...] = (xn * (1.0 + scp[0]) + shp[0]).astype(out.dtype)

        @pl.when(i >= n_ptiles)
        def _():
            out[...] = (xn * (1.0 + scs[...]) + shs[...]).astype(out.dtype)


def _normmod(x, mods, rows_per_seq, n_prompt_rows, tr):
    m, d = x.shape
    n_ptiles = n_prompt_rows // tr
    tiles_per_seq = rows_per_seq // tr
    n_seq = mods[0][0].shape[0]
    pspec = pl.BlockSpec((1, 1, d), lambda i: (jnp.minimum(i // tiles_per_seq, n_seq - 1), 0, 0))
    sspec = pl.BlockSpec((tr, d), lambda i: (jnp.maximum(i - n_ptiles, 0), 0))
    in_specs = [pl.BlockSpec((tr, d), lambda i: (i, 0))]
    args = [x]
    for (shp, scp, shs, scs) in mods:
        in_specs += [pspec, pspec, sspec, sspec]
        args += [shp, scp, shs, scs]
    outs = pl.pallas_call(
        functools.partial(_normmod_kernel, len(mods), n_ptiles),
        out_shape=[jax.ShapeDtypeStruct((m, d), BF16)] * len(mods),
        grid=(m // tr,),
        in_specs=in_specs,
        out_specs=[pl.BlockSpec((tr, d), lambda i: (i, 0))] * len(mods),
        compiler_params=_cparams(1),
        name="normmod",
    )(*args)
    return outs


def _rmsgain_kernel(x_ref, g_ref, o_ref):
    o_ref[...] = (_rms(x_ref[...]) * g_ref[...]).astype(o_ref.dtype)


def _rmsgain(x, g, width, out_dtype, tr):
    m = x.shape[0]
    return pl.pallas_call(
        _rmsgain_kernel,
        out_shape=jax.ShapeDtypeStruct((m, width), out_dtype),
        grid=(m // tr,),
        in_specs=[pl.BlockSpec((tr, width), lambda i: (i, 0)),
                  pl.BlockSpec((1, width), lambda i: (0, 0))],
        out_specs=pl.BlockSpec((tr, width), lambda i: (i, 0)),
        compiler_params=_cparams(1),
        name="rmsgain",
    )(x, g.reshape(1, width))


def _rmsgain_split_kernel(n_ptiles, x_ref, g_ref, op_ref, os_ref):
    y = _rms(x_ref[...]) * g_ref[...]
    i = pl.program_id(0)

    @pl.when(i < n_ptiles)
    def _():
        op_ref[...] = y

    @pl.when(i >= n_ptiles)
    def _():
        os_ref[...] = y


def _rmsgain_split(x, g, n_prompt_rows, tr):
    m, d = x.shape
    n_ptiles = n_prompt_rows // tr
    return pl.pallas_call(
        functools.partial(_rmsgain_split_kernel, n_ptiles),
        out_shape=[jax.ShapeDtypeStruct((n_prompt_rows, d), F32),
                   jax.ShapeDtypeStruct((m - n_prompt_rows, d), F32)],
        grid=(m // tr,),
        in_specs=[pl.BlockSpec((tr, d), lambda i: (i, 0)),
                  pl.BlockSpec((1, d), lambda i: (0, 0))],
        out_specs=[pl.BlockSpec((tr, d), lambda i: (jnp.minimum(i, n_ptiles - 1), 0)),
                   pl.BlockSpec((tr, d), lambda i: (jnp.maximum(i - n_ptiles, 0), 0))],
        compiler_params=_cparams(1),
        name="rmsgain_split",
    )(x, g.reshape(1, d))


def _wdot(x_ref, w_ref, wb_ref):
    @pl.when(pl.program_id(1) == 0)
    def _():
        wb_ref[...] = w_ref[...].astype(BF16)

    return jnp.dot(x_ref[...], wb_ref[...], preferred_element_type=F32)


def _mm_plain_kernel(x_ref, w_ref, o_ref, wb_ref):
    o_ref[...] = _wdot(x_ref, w_ref, wb_ref).astype(o_ref.dtype)


def _mm_rope_pairs_kernel(n_q, n_rope, q_scale, half, x_ref, w_ref, cos_ref, sin_ref, o_ref, wb_ref):
    acc = _wdot(x_ref, w_ref, wb_ref)
    n = pl.program_id(0)
    tn = acc.shape[1]
    sc = jnp.where(n < n_q, q_scale, 1.0).astype(F32)
    c = jnp.where(n < n_rope, cos_ref[...], 1.0) * sc
    s = jnp.where(n < n_rope, sin_ref[...], 0.0) * sc
    for h in range(tn // (2 * half)):
        a, b = h * 2 * half, h * 2 * half + half
        x1, x2 = acc[:, a:b], acc[:, b:b + half]
        o_ref[:, a:b] = (x1 * c - x2 * s).astype(o_ref.dtype)
        o_ref[:, b:b + half] = (x2 * c + x1 * s).astype(o_ref.dtype)


def _mm_rope_lanes_kernel(n_plain, half, x_ref, w_ref, cos_ref, sin_ref, o_ref, wb_ref):
    acc = _wdot(x_ref, w_ref, wb_ref)
    n = pl.program_id(0)
    tm, tn = acc.shape

    c = jnp.where(n >= n_plain, cos_ref[...], 1.0)
    s = jnp.where(n >= n_plain, sin_ref[...], 0.0)
    lane = lax.broadcasted_iota(jnp.int32, (tm, LANES), 1)
    first = lax.rem(lane, 2 * half) < half
    for j in range(tn // LANES):
        x = acc[:, j * LANES:(j + 1) * LANES]
        rot = jnp.where(first, pltpu.roll(x, LANES - half, 1), pltpu.roll(x, half, 1))
        o_ref[:, j * LANES:(j + 1) * LANES] = (x * c + rot * s).astype(o_ref.dtype)


def _mm_residual_kernel(n_ptiles, x_ref, w_ref, res_ref, gp_ref, gs_ref, o_ref, wb_ref):
    acc = _wdot(x_ref, w_ref, wb_ref)
    gate = jnp.where(pl.program_id(1) < n_ptiles, gp_ref[0], gs_ref[...])
    o_ref[...] = res_ref[...] + gate * acc


def _mm_call(kernel_fn, x, w, extra_args, extra_specs, out_dtype, tm, tn, name):
    m, k = x.shape
    n = w.shape[1]
    single = 2 * k * tn * w.dtype.itemsize > VMEM_LIMIT_BYTES // 2
    w_spec = pl.BlockSpec((k, tn), lambda j, i: (0, j), **({"pipeline_mode": pl.Buffered(1)} if single else {}))
    return pl.pallas_call(
        kernel_fn,
        out_shape=jax.ShapeDtypeStruct((m, n), out_dtype),
        grid=(n // tn, m // tm),
        in_specs=[pl.BlockSpec((tm, k), lambda j, i: (i, 0)), w_spec] + extra_specs,
        out_specs=pl.BlockSpec((tm, tn), lambda j, i: (i, j)),
        scratch_shapes=[pltpu.VMEM((k, tn), BF16)],
        compiler_params=_cparams(2),
        name=name,
    )(x, w, *extra_args)


def _mm_plain(x, w, out_dtype, tm, tn):
    return _mm_call(_mm_plain_kernel, x, w, [], [], out_dtype, tm, tn, "mm_plain")


def _mm_rope_pairs(x, w, cos, sin, n_q, n_rope, q_scale, out_dtype, tm, tn):
    half = cos.shape[1]
    tspec = pl.BlockSpec((tm, half), lambda j, i: (i, 0))
    return _mm_call(functools.partial(_mm_rope_pairs_kernel, n_q, n_rope, q_scale, half),
                    x, w, [cos, sin], [tspec, tspec], out_dtype, tm, tn, "mm_rope_pairs")


def _mm_rope_lanes(x, w, cos, sin, n_plain, half, out_dtype, tm, tn):
    tspec = pl.BlockSpec((tm, LANES), lambda j, i: (i, 0))
    return _mm_call(functools.partial(_mm_rope_lanes_kernel, n_plain, half),
                    x, w, [cos, sin], [tspec, tspec], out_dtype, tm, tn, "mm_rope_lanes")


def _mm_residual(x, w, res, gate_p, gate_s, rows_per_seq, n_prompt_rows, tm, tn):
    n_ptiles = n_prompt_rows // tm
    tiles_per_seq = rows_per_seq // tm
    n_seq = gate_p.shape[0]
    specs = [pl.BlockSpec((tm, tn), lambda j, i: (i, j)),
             pl.BlockSpec((1, 1, tn), lambda j, i: (jnp.minimum(i // tiles_per_seq, n_seq - 1), 0, j)),
             pl.BlockSpec((tm, tn), lambda j, i: (jnp.maximum(i - n_ptiles, 0), j))]
    return _mm_call(functools.partial(_mm_residual_kernel, n_ptiles),
                    x, w, [res, gate_p, gate_s], specs, F32, tm, tn, "mm_residual")


def _ret_prompt_kernel(n_chunks, lg_ref, gc_ref, q_ref, k_ref, v_ref, g_ref, gn_ref,
                       o_ref, sout_ref, s_ref):
    h, t = pl.program_id(1), pl.program_id(2)
    lg, gc = lg_ref[h], gc_ref[h]
    c = R_CHUNK

    @pl.when(t == 0)
    def _():
        s_ref[...] = jnp.zeros_like(s_ref)

    ri = lax.broadcasted_iota(jnp.int32, (c, c), 0)
    ci = lax.broadcasted_iota(jnp.int32, (c, c), 1)
    diff = (ri - ci).astype(F32)
    decay = jnp.where(diff >= 0, jnp.exp(lg * jnp.maximum(diff, 0.0)), 0.0)
    idx = lax.broadcasted_iota(jnp.int32, (c, 1), 0).astype(F32)
    q_decay = jnp.exp(lg * (idx + 1.0))
    k_decay = jnp.exp(lg * (c - 1.0 - idx))
    gn = gn_ref[...]

    for j in range(n_chunks):
        rows = slice(j * c, (j + 1) * c)
        q, k, v = q_ref[rows, :], k_ref[rows, :], v_ref[rows, :]
        state = s_ref[...]
        scores = lax.dot_general(q, k, NT_DIMS, preferred_element_type=F32) * decay
        o = (jnp.dot(scores.astype(BF16), v, preferred_element_type=F32)
             + jnp.dot(q, state.astype(BF16), preferred_element_type=F32) * q_decay)
        kd_t = (k.astype(F32) * k_decay).T.astype(BF16)
        s_ref[...] = gc * state + jnp.dot(kd_t, v, preferred_element_type=F32)
        o_ref[rows, :] = (_rms(o) * gn * _silu(g_ref[rows, :].astype(F32))).astype(o_ref.dtype)

    @pl.when(t == pl.num_programs(2) - 1)
    def _():
        sout_ref[0, 0] = s_ref[...]


def _ret_prompt(proj, gn_g, log_g, n_seq, seq_len, heads, dk, dv, tb):
    m = proj.shape[0]
    nt = seq_len // tb
    gc = jnp.exp(log_g * R_CHUNK)
    kcol, vcol, gcol = heads, (2 * heads * dk) // dv, (2 * heads * dk) // dv + heads

    def rowmap(col0):
        return lambda b, h, t, *_: (b * nt + t, col0 + h)

    grid_spec = pltpu.PrefetchScalarGridSpec(
        num_scalar_prefetch=2,
        grid=(n_seq, heads, nt),
        in_specs=[pl.BlockSpec((tb, dk), rowmap(0)),
                  pl.BlockSpec((tb, dk), rowmap(kcol)),
                  pl.BlockSpec((tb, dv), rowmap(vcol)),
                  pl.BlockSpec((tb, dv), rowmap(gcol)),
                  pl.BlockSpec((1, dv), lambda b, h, t, *_: (0, h))],
        out_specs=[pl.BlockSpec((tb, dv), rowmap(0)),
                   pl.BlockSpec((1, 1, dk, dv), lambda b, h, t, *_: (b, h, 0, 0))],
        scratch_shapes=[pltpu.VMEM((dk, dv), F32)],
    )
    return pl.pallas_call(
        functools.partial(_ret_prompt_kernel, tb // R_CHUNK),
        out_shape=[jax.ShapeDtypeStruct((m, heads * dv), BF16),
                   jax.ShapeDtypeStruct((n_seq, heads, dk, dv), F32)],
        grid_spec=grid_spec,
        compiler_params=_cparams(3),
        name="ret_prompt",
    )(log_g, gc, proj, proj, proj, proj, gn_g.reshape(1, heads * dv))


def _ret_sample_kernel(bb, t_len, lg_ref, gc_ref, q_ref, k_ref, v_ref, g_ref, gn_ref, s_ref, *refs):
    o_ref, sout_ref = refs[-2:]
    h = pl.program_id(1)
    lg, gc = lg_ref[h], gc_ref[h]
    r = bb * t_len
    ri = lax.broadcasted_iota(jnp.int32, (r, r), 0)
    ci = lax.broadcasted_iota(jnp.int32, (r, r), 1)
    diff = (ri - ci).astype(F32)
    same_seq = (ri // t_len) == (ci // t_len)
    decay = jnp.where(same_seq & (diff >= 0), jnp.exp(lg * jnp.maximum(diff, 0.0)), 0.0)
    pos = lax.rem(lax.broadcasted_iota(jnp.int32, (r, 1), 0), t_len).astype(F32)
    q_decay = jnp.exp(lg * (pos + 1.0))
    k_decay = jnp.exp(lg * (t_len - 1.0 - pos))
    row_seq = lax.broadcasted_iota(jnp.int32, (r, 1), 0) // t_len
    col_seq = lax.broadcasted_iota(jnp.int32, (1, r), 1) // t_len

    q, k, v = q_ref[...], k_ref[...], v_ref[...]
    scores = lax.dot_general(q, k, NT_DIMS, preferred_element_type=F32) * decay
    o = jnp.dot(scores.astype(BF16), v, preferred_element_type=F32)
    kd_t = (k.astype(F32) * k_decay).T
    cross = jnp.zeros_like(o)
    for b in range(bb):
        state = s_ref[b, 0]
        cross_b = jnp.dot(q, state.astype(BF16), preferred_element_type=F32)
        cross = jnp.where(row_seq == b, cross_b, cross)
        kd_b = jnp.where(col_seq == b, kd_t, 0.0).astype(BF16)
        sout_ref[b, 0] = (gc * state + jnp.dot(kd_b, v, preferred_element_type=F32)).astype(sout_ref.dtype)
    o = o + cross * q_decay
    o_ref[...] = (_rms(o) * gn_ref[...] * _silu(g_ref[...].astype(F32))).astype(o_ref.dtype)


def _ret_sample(proj, o_all, states, new_states, layer, gn_g, log_g, n_prompt_rows, t_len, heads, dk, dv, bb):
    n_seq = states.shape[1]
    r = bb * t_len
    row0 = n_prompt_rows // r
    gc = jnp.exp(log_g * t_len)
    kcol, vcol, gcol = heads, (2 * heads * dk) // dv, (2 * heads * dk) // dv + heads

    def rowmap(col0):
        return lambda i, h, *_: (row0 + i, col0 + h)

    sspec = pl.BlockSpec((None, bb, 1, dk, dv), lambda i, h, *_: (layer, i, h, 0, 0))
    any_spec = pl.BlockSpec(memory_space=pl.ANY)
    args = [log_g, gc, proj, proj, proj, proj, gn_g.reshape(1, heads * dv), states, o_all]
    aliases = {8: 0}
    if new_states is not None:
        args.append(new_states)
        aliases[9] = 1
    grid_spec = pltpu.PrefetchScalarGridSpec(
        num_scalar_prefetch=2,
        grid=(n_seq // bb, heads),
        in_specs=[pl.BlockSpec((r, dk), rowmap(0)),
                  pl.BlockSpec((r, dk), rowmap(kcol)),
                  pl.BlockSpec((r, dv), rowmap(vcol)),
                  pl.BlockSpec((r, dv), rowmap(gcol)),
                  pl.BlockSpec((1, dv), lambda i, h, *_: (0, h)),
                  sspec] + [any_spec] * (len(args) - 8),
        out_specs=[pl.BlockSpec((r, dv), rowmap(0)), sspec],
    )
    return pl.pallas_call(
        functools.partial(_ret_sample_kernel, bb, t_len),
        out_shape=[jax.ShapeDtypeStruct(o_all.shape, o_all.dtype),
                   jax.ShapeDtypeStruct(states.shape, states.dtype)],
        grid_spec=grid_spec,
        input_output_aliases=aliases,
        compiler_params=_cparams(2),
        name="ret_sample",
    )(*args)


def _kv_down_kernel(n_ptiles, x_ref, wl_ref, wr_ref, ng_ref, cos_ref, sin_ref,
                    latp_ref, krp_ref, lats_ref, krs_ref, latb_ref, krb_ref, wlb_ref, wrb_ref):
    i = pl.program_id(0)

    @pl.when(i == 0)
    def _():
        wlb_ref[...] = wl_ref[...].astype(BF16)
        wrb_ref[...] = wr_ref[...].astype(BF16)

    x = x_ref[...]
    lat = _rms(jnp.dot(x, wlb_ref[...], preferred_element_type=F32)) * ng_ref[...]
    c = jnp.dot(x, wrb_ref[...], preferred_element_type=F32)
    half = c.shape[1] // 2
    rot = jnp.concatenate([c[:, half:], c[:, :half]], axis=-1)
    kr = c * cos_ref[...] + rot * sin_ref[...]
    latb_ref[...] = lat.astype(BF16)
    krb_ref[...] = kr.astype(BF16)

    @pl.when(i < n_ptiles)
    def _():
        latp_ref[...] = lat
        krp_ref[...] = kr

    @pl.when(i >= n_ptiles)
    def _():
        lats_ref[...] = lat
        krs_ref[...] = kr


def _kv_down(xn, w_down, norm_g, cos, sin, lora, n_prompt_rows, tm):
    m, d = xn.shape
    rope = w_down.shape[1] - lora
    n_ptiles = n_prompt_rows // tm
    w_lat, w_rope = w_down[:, :lora], w_down[:, lora:]
    row = lambda i: (i, 0)
    fixed = lambda i: (0, 0)
    prow = lambda i: (jnp.minimum(i, n_ptiles - 1), 0)
    srow = lambda i: (jnp.maximum(i - n_ptiles, 0), 0)
    ms = m - n_prompt_rows
    return pl.pallas_call(
        functools.partial(_kv_down_kernel, n_ptiles),
        out_shape=[jax.ShapeDtypeStruct((n_prompt_rows, lora), F32), jax.ShapeDtypeStruct((n_prompt_rows, rope), F32),
                   jax.ShapeDtypeStruct((ms, lora), F32), jax.ShapeDtypeStruct((ms, rope), F32),
                   jax.ShapeDtypeStruct((m, lora), BF16), jax.ShapeDtypeStruct((m, rope), BF16)],
        grid=(m // tm,),
        in_specs=[pl.BlockSpec((tm, d), row), pl.BlockSpec((d, lora), fixed), pl.BlockSpec((d, rope), fixed),
                  pl.BlockSpec((1, lora), fixed), pl.BlockSpec((tm, rope), row), pl.BlockSpec((tm, rope), row)],
        out_specs=[pl.BlockSpec((tm, lora), prow), pl.BlockSpec((tm, rope), prow),
                   pl.BlockSpec((tm, lora), srow), pl.BlockSpec((tm, rope), srow),
                   pl.BlockSpec((tm, lora), row), pl.BlockSpec((tm, rope), row)],
        scratch_shapes=[pltpu.VMEM((d, lora), BF16), pltpu.VMEM((d, rope), BF16)],
        compiler_params=_cparams(1),
        name="kv_down",
    )(xn, w_lat, w_rope, norm_g.reshape(1, lora), cos, sin)


def _head_in_kernel(x_ref, w_ref, o_ref):
    o_ref[0] = jnp.dot(x_ref[...], w_ref[0].astype(BF16), preferred_element_type=F32).astype(o_ref.dtype)


def _head_in(x, w, row0, rows):
    heads, kin, nout = w.shape
    rb = row0 // rows
    return pl.pallas_call(
        _head_in_kernel,
        out_shape=jax.ShapeDtypeStruct((heads, rows, nout), BF16),
        grid=(heads,),
        in_specs=[pl.BlockSpec((rows, kin), lambda h: (rb, h)),
                  pl.BlockSpec((1, kin, nout), lambda h: (h, 0, 0))],
        out_specs=pl.BlockSpec((1, rows, nout), lambda h: (h, 0, 0)),
        compiler_params=_cparams(1),
        name="head_in",
    )(x, w)


def _head_out_kernel(x_ref, w_ref, g_ref, oin_ref, o_ref):
    del oin_ref
    o = jnp.dot(x_ref[0], w_ref[0].astype(BF16), preferred_element_type=F32)
    o_ref[...] = (o * _silu(g_ref[...])).astype(o_ref.dtype)


def _head_out(x, w, g, g_col0, o_all, row0):
    heads, rows, kin = x.shape
    nout = w.shape[2]
    gblk = g_col0 // nout
    rb = row0 // rows
    return pl.pallas_call(
        _head_out_kernel,
        out_shape=jax.ShapeDtypeStruct(o_all.shape, o_all.dtype),
        grid=(heads,),
        in_specs=[pl.BlockSpec((1, rows, kin), lambda h: (h, 0, 0)),
                  pl.BlockSpec((1, kin, nout), lambda h: (h, 0, 0)),
                  pl.BlockSpec((rows, nout), lambda h: (rb, gblk + h)),
                  pl.BlockSpec(memory_space=pl.ANY)],
        out_specs=pl.BlockSpec((rows, nout), lambda h: (rb, h)),
        input_output_aliases={3: 0},
        compiler_params=_cparams(1),
        name="head_out",
    )(x, w, g, o_all)


def _lanes(x, n):
    return jnp.tile(x, (1, n // LANES)) if n >= LANES else x[:, :n]


def _softmax_update(s, v, c, m_ref, l_ref, acc_ref):
    m_prev = m_ref[...]
    m_new = jnp.maximum(m_prev, jnp.max(s, axis=-1, keepdims=True))
    p = jnp.exp2((s - _lanes(m_new, s.shape[1])) * c)
    alpha = jnp.exp2((m_prev - m_new) * c)
    l_ref[...] = alpha * l_ref[...] + jnp.sum(p, axis=-1, keepdims=True)
    acc_ref[...] = (_lanes(alpha, acc_ref.shape[1]) * acc_ref[...]
                    + jnp.dot(p.astype(BF16), v, preferred_element_type=F32))
    m_ref[...] = m_new


def _softmax_init(m_ref, l_ref, acc_ref):
    m_ref[...] = jnp.full_like(m_ref, -jnp.inf)
    l_ref[...] = jnp.zeros_like(l_ref)
    acc_ref[...] = jnp.zeros_like(acc_ref)


def _attn_prompt_kernel(g, c, qi_ref, ki_ref, qn_ref, qr_ref, wuk_ref, k_ref, kr_ref, wuv_ref, gate_ref,
                        o_ref, ql_s, qr_s, m_s, l_s, acc_s):
    t = pl.program_id(2)
    qi, ki = qi_ref[t], ki_ref[t]
    tq = qn_ref.shape[0]
    nope, rope, dv = qn_ref.shape[1] // g, qr_ref.shape[1] // g, o_ref.shape[1] // g
    tk = k_ref.shape[0]

    @pl.when(ki == 0)
    def _():
        for h in range(g):
            rows = pl.ds(h * tq, tq)
            ql_s[rows, :] = jnp.dot(qn_ref[:, h * nope:(h + 1) * nope], wuk_ref[h].astype(BF16),
                                    preferred_element_type=F32).astype(BF16)
            qr_s[rows, :] = qr_ref[:, h * rope:(h + 1) * rope]
        _softmax_init(m_s, l_s, acc_s)

    def update(masked):
        k, kr = k_ref[...], kr_ref[...]
        if masked:
            visible = (lax.broadcasted_iota(jnp.int32, (tq, tk), 1)
                       <= lax.broadcasted_iota(jnp.int32, (tq, tk), 0))
        for h in range(g):
            rows = pl.ds(h * tq, tq)
            s = (lax.dot_general(ql_s[rows, :], k, NT_DIMS, preferred_element_type=F32)
                 + lax.dot_general(qr_s[rows, :], kr, NT_DIMS, preferred_element_type=F32))
            if masked:
                s = jnp.where(visible, s, -jnp.inf)
            _softmax_update(s, k, c, m_s.at[rows, :], l_s.at[rows, :], acc_s.at[rows, :])

    @pl.when(ki < qi)
    def _():
        update(False)

    @pl.when(ki == qi)
    def _():
        update(True)
        for h in range(g):
            rows = pl.ds(h * tq, tq)
            o_lat = acc_s[rows, :] * _lanes(1.0 / l_s[rows, :], acc_s.shape[1])
            o = jnp.dot(o_lat.astype(BF16), wuv_ref[h].astype(BF16), preferred_element_type=F32)
            cols = slice(h * dv, (h + 1) * dv)
            o_ref[:, cols] = (o * _silu(gate_ref[:, cols])).astype(o_ref.dtype)


def _attn_prompt(q, w_uk_h, lat, kr, w_uv_h, gate, gate_col0, n_seq, seq_len, scale, g, tq):
    m = q.shape[0]
    heads, nope, lora = w_uk_h.shape
    dv = w_uv_h.shape[2]
    rope = kr.shape[1]
    nq = seq_len // tq
    pairs = [(i, j) for i in range(nq) for j in range(i + 1)]
    qi_tab = jnp.asarray([p[0] for p in pairs], jnp.int32)
    ki_tab = jnp.asarray([p[1] for p in pairs], jnp.int32)
    rope_blk0 = heads * nope // (g * rope)
    gate_blk0 = gate_col0 // (g * dv)
    assert heads * nope % (g * rope) == 0 and gate_col0 % (g * dv) == 0

    def qrow(col0):
        return lambda b, hg, t, qi, ki: (b * nq + qi[t], col0 + hg)

    krow = lambda b, hg, t, qi, ki: (b * nq + ki[t], 0)
    hmap = lambda b, hg, t, qi, ki: (hg, 0, 0)
    grid_spec = pltpu.PrefetchScalarGridSpec(
        num_scalar_prefetch=2,
        grid=(n_seq, heads // g, len(pairs)),
        in_specs=[pl.BlockSpec((tq, g * nope), qrow(0)),
                  pl.BlockSpec((tq, g * rope), qrow(rope_blk0)),
                  pl.BlockSpec((g, nope, lora), hmap),
                  pl.BlockSpec((tq, lora), krow),
                  pl.BlockSpec((tq, rope), krow),
                  pl.BlockSpec((g, lora, dv), hmap),
                  pl.BlockSpec((tq, g * dv), qrow(gate_blk0))],
        out_specs=pl.BlockSpec((tq, g * dv), qrow(0)),
        scratch_shapes=[pltpu.VMEM((g * tq, lora), BF16), pltpu.VMEM((g * tq, rope), BF16),
                        pltpu.VMEM((g * tq, LANES), F32), pltpu.VMEM((g * tq, LANES), F32),
                        pltpu.VMEM((g * tq, lora), F32)],
    )
    return pl.pallas_call(
        functools.partial(_attn_prompt_kernel, g, scale * math.log2(math.e)),
        out_shape=jax.ShapeDtypeStruct((m, heads * dv), BF16),
        grid_spec=grid_spec,
        compiler_params=_cparams(3),
        name="attn_prompt",
    )(qi_tab, ki_tab, q, q, w_uk_h, lat, kr, w_uv_h, gate)


def _attn_sample_kernel(ns, pg, t_len, c, pt_ref, ql_ref, qr_ref, kn_ref, krn_ref, lat_hbm, krt_hbm,
                        o_ref, lat_buf, krt_buf, lat_sem, krt_sem, kb_s, krb_s, m_s, l_s, acc_s):
    i, j = pl.program_id(0), pl.program_id(1)
    n_steps = pl.num_programs(1)
    last = pl.num_programs(0) * n_steps - 1
    g = i * n_steps + j
    slot = lax.rem(g, 2)
    page = lat_buf.shape[2]

    def page_copies(step, into):
        seq0, first = (step // n_steps) * ns, lax.rem(step, n_steps) * pg
        out = []
        for b in range(ns):
            for p in range(pg):
                pid = pt_ref[(seq0 + b) * (n_steps * pg) + first + p]
                out.append(pltpu.make_async_copy(lat_hbm.at[pid], lat_buf.at[into, b * pg + p], lat_sem.at[into]))
                out.append(pltpu.make_async_copy(krt_hbm.at[pid], krt_buf.at[into, b * pg + p], krt_sem.at[into]))
        return out

    @pl.when(j == 0)
    def _():
        _softmax_init(m_s, l_s, acc_s)

    @pl.when(g == 0)
    def _():
        for cp in page_copies(g, slot):
            cp.start()

    for cp in page_copies(g, slot):
        cp.wait()
    nxt = jnp.minimum(g + 1, last)
    for cp in page_copies(nxt, 1 - slot):
        cp.start()

    for b in range(ns):
        for p in range(pg):
            kb_s[b, p * page:(p + 1) * page, :] = lat_buf[slot, b * pg + p].astype(BF16)
            krb_s[b, :, p * page:(p + 1) * page] = krt_buf[slot, b * pg + p].astype(BF16)
    scores = [lax.dot_general(ql_ref[b], kb_s[b], NT_DIMS, preferred_element_type=F32)
              + jnp.dot(qr_ref[b], krb_s[b], preferred_element_type=F32) for b in range(ns)]
    for b in range(ns):
        _softmax_update(scores[b], kb_s[b], c, m_s.at[b], l_s.at[b], acc_s.at[b])

    @pl.when(g == last)
    def _():
        for cp in page_copies(nxt, 1 - slot):
            cp.wait()

    @pl.when(j == n_steps - 1)
    def _():
        rows, nk = ql_ref.shape[1], kn_ref.shape[1]
        tok = lax.rem(lax.broadcasted_iota(jnp.int32, (rows, nk), 0), t_len)
        visible = lax.broadcasted_iota(jnp.int32, (rows, nk), 1) <= tok
        for b in range(ns):
            kn, krn = kn_ref[b], krn_ref[b]
            s = (lax.dot_general(ql_ref[b], kn, NT_DIMS, preferred_element_type=F32)
                 + lax.dot_general(qr_ref[b], krn, NT_DIMS, preferred_element_type=F32))
            _softmax_update(jnp.where(visible, s, -jnp.inf), kn, c, m_s.at[b], l_s.at[b], acc_s.at[b])
            o_ref[b] = (acc_s[b] * _lanes(1.0 / l_s[b], acc_s.shape[2])).astype(o_ref.dtype)


def _attn_sample(q_lat, q_rope, lat_new, kr_new, cache_lat, cache_kr, page_table, t_len, scale, ns, pg):
    n_seq, rows, lora = q_lat.shape
    rope = q_rope.shape[2]
    page = cache_lat.shape[1]
    n_pages = page_table.shape[1]
    nk = lat_new.shape[1]
    seqmap = lambda i, j, pt: (i, 0, 0)
    hbm = pl.BlockSpec(memory_space=pl.ANY)
    cache_krt = jnp.swapaxes(cache_kr, 1, 2)
    grid_spec = pltpu.PrefetchScalarGridSpec(
        num_scalar_prefetch=1,
        grid=(n_seq // ns, n_pages // pg),
        in_specs=[pl.BlockSpec((ns, rows, lora), seqmap), pl.BlockSpec((ns, rows, rope), seqmap),
                  pl.BlockSpec((ns, nk, lora), seqmap), pl.BlockSpec((ns, nk, rope), seqmap), hbm, hbm],
        out_specs=pl.BlockSpec((ns, rows, lora), seqmap),
        scratch_shapes=[pltpu.VMEM((2, ns * pg, page, lora), cache_lat.dtype),
                        pltpu.VMEM((2, ns * pg, rope, page), cache_kr.dtype),
                        pltpu.SemaphoreType.DMA((2,)), pltpu.SemaphoreType.DMA((2,)),
                        pltpu.VMEM((ns, pg * page, lora), BF16), pltpu.VMEM((ns, rope, pg * page), BF16),
                        pltpu.VMEM((ns, rows, LANES), F32), pltpu.VMEM((ns, rows, LANES), F32),
                        pltpu.VMEM((ns, rows, lora), F32)],
    )
    return pl.pallas_call(
        functools.partial(_attn_sample_kernel, ns, pg, t_len, scale * math.log2(math.e)),
        out_shape=jax.ShapeDtypeStruct((n_seq, rows, lora), BF16),
        grid_spec=grid_spec,
        compiler_params=_cparams(2),
        name="attn_sample",
    )(page_table.reshape(-1), q_lat, q_rope, lat_new, kr_new, cache_lat, cache_krt)


def _rope_angles(pos, half):
    inv = jnp.exp(-LOG_ROPE_BASE * jnp.arange(half, dtype=F32) / half)
    ang = pos[:, None] * inv[None, :]
    return jnp.cos(ang), jnp.sin(ang)


def _col_tile(*widths, pref=512):
    t = pref
    while any(w % t for w in widths):
        t //= 2
    assert t % LANES == 0
    return t


def _row_tile(m, pref):
    return max(t for t in range(BF16_SUBLANES, pref + 1, BF16_SUBLANES) if m % t == 0)


def _split_mod(mod, n, n_prompt_seq, n_sample_seq, t_sample):
    d = mod.shape[1] // n
    out = []
    for i in range(n):
        v = mod[:, i * d:(i + 1) * d]
        vp = v[:n_prompt_seq].reshape(n_prompt_seq, 1, d)
        vs = jnp.repeat(v[n_prompt_seq:n_prompt_seq + n_sample_seq], t_sample, axis=0)
        out.append((vp, vs))
    return out


def kernel(x_prompt, x_sample, c_prompt, c_sample, state_retention, cache_kv_latent, cache_k_rope, page_table, ret_w_mod, ret_b_mod, ret_w_in, ret_gn_g, ret_w_out, kv_w_mod, kv_b_mod, kv_w_down, kv_norm_g, kv_w_uk, kv_w_uv, mla_w_mod, mla_b_mod, mla_w_in, mla_q_norm_g, mla_w_qb, mla_w_o, final_norm_g):
    bp, t_p, d = x_prompt.shape
    bs, t_s, _ = x_sample.shape
    n_ret, _, r_heads, r_dk, r_dv = state_retention.shape
    n_mla = mla_w_mod.shape[0]
    lora, m_heads, nope = kv_w_uk.shape
    m_dv = kv_w_uv.shape[2]
    rope = cache_k_rope.shape[2]
    q_lora = mla_q_norm_g.shape[1]
    page = cache_kv_latent.shape[1]
    past_len = page_table.shape[1] * page
    mp, ms = bp * t_p, bs * t_s
    m = mp + ms

    tm = ms
    assert t_p % tm == 0 and tm % BF16_SUBLANES == 0
    tm_any = _row_tile(m, 1100)
    tr = min(256, tm)
    tn = 512
    tb = min(512, t_p)
    assert t_p % tb == 0 and tb % R_CHUNK == 0
    bb = 8
    assert bs % bb == 0 and (bb * t_s) % BF16_SUBLANES == 0 and mp % (bb * t_s) == 0
    assert r_dk == 2 * LANES and r_dv == 2 * r_dk and tn % r_dv == 0
    assert rope * 2 == LANES and nope == LANES and m_dv == LANES

    pos = jnp.concatenate([jnp.tile(jnp.arange(t_p, dtype=F32), bp),
                           jnp.tile(past_len + jnp.arange(t_s, dtype=F32), bs)])
    cos_r, sin_r = _rope_angles(pos, r_dk // 2)
    cos_m, sin_m = _rope_angles(pos, rope // 2)
    cos_kr = jnp.concatenate([cos_m, cos_m], axis=1)
    sin_kr = jnp.concatenate([-sin_m, sin_m], axis=1)
    cos_q = jnp.concatenate([cos_kr, cos_kr], axis=1)
    sin_q = jnp.concatenate([sin_kr, sin_kr], axis=1)

    log_g = jnp.log1p(-jnp.exp2(-5.0 - jnp.arange(r_heads, dtype=F32)))

    n_cond = bp + bs
    c_all = jnp.concatenate([c_prompt, c_sample], axis=0)
    c_all = jnp.pad(c_all, ((0, (-n_cond) % BF16_SUBLANES), (0, 0)))
    h = jnp.concatenate([x_prompt.reshape(mp, d), x_sample.reshape(ms, d)], axis=0)

    split = functools.partial(_split_mod, n_prompt_seq=bp, n_sample_seq=bs, t_sample=t_s)
    ret_p, ret_s = [], None
    for a in range(n_ret):
        (sh_p, sh_s), (sc_p, sc_s), (gt_p, gt_s) = split(_adaln(c_all, ret_w_mod[a], ret_b_mod[a]), 3)
        (xn,) = _normmod(h, [(sh_p, sc_p, sh_s, sc_s)], t_p, mp, tr)
        hk = r_heads * r_dk
        proj = _mm_rope_pairs(xn, ret_w_in[a], cos_r, sin_r, hk // tn, 2 * hk // tn, r_dk ** -0.5, BF16, tm_any, tn)
        o_all, s_p = _ret_prompt(proj, ret_gn_g[a], log_g, bp, t_p, r_heads, r_dk, r_dv, tb)
        o_all, ret_s = _ret_sample(proj, o_all, state_retention, ret_s, a, ret_gn_g[a], log_g, mp, t_s,
                                   r_heads, r_dk, r_dv, bb)
        h = _mm_residual(o_all, ret_w_out[a], h, gt_p, gt_s, t_p, mp, tm, _col_tile(d))
        ret_p.append(s_p)

    lat_p = kr_p = lat_s = kr_s = None
    for b in range(n_mla):
        (sh_p, sh_s), (sc_p, sc_s), (gt_p, gt_s) = split(_adaln(c_all, mla_w_mod[b], mla_b_mod[b]), 3)
        if b == 0:
            (ksh_p, ksh_s), (ksc_p, ksc_s) = split(_adaln(c_all, kv_w_mod, kv_b_mod), 2)
            xn, xn_kv = _normmod(h, [(sh_p, sc_p, sh_s, sc_s), (ksh_p, ksc_p, ksh_s, ksc_s)], t_p, mp, tr)
            lat_p, kr_p, lat_s, kr_s, lat_b, kr_b = _kv_down(xn_kv, kv_w_down, kv_norm_g, cos_kr, sin_kr,
                                                             lora, mp, tm)
            pad = ((0, 0), (0, BF16_SUBLANES - t_s), (0, 0))
            lat_new = jnp.pad(lat_b[mp:].reshape(bs, t_s, lora), pad)
            kr_new = jnp.pad(kr_b[mp:].reshape(bs, t_s, rope), pad)
            w_uk_h = jnp.transpose(kv_w_uk, (1, 2, 0))
            w_uv_h = jnp.transpose(kv_w_uv, (1, 0, 2))
        else:
            (xn,) = _normmod(h, [(sh_p, sc_p, sh_s, sc_s)], t_p, mp, tr)
        proj = _mm_plain(xn, mla_w_in[b], F32, tm_any, _col_tile(mla_w_in.shape[2]))
        qa = _rmsgain(proj, mla_q_norm_g[b], q_lora, BF16, tr)
        w_qb = mla_w_qb[b].reshape(q_lora, m_heads, nope + rope)
        w_qb = jnp.concatenate([w_qb[:, :, :nope].reshape(q_lora, m_heads * nope),
                                w_qb[:, :, nope:].reshape(q_lora, m_heads * rope)], axis=1)
        tq_n = _col_tile(m_heads * nope, m_heads * rope)
        q = _mm_rope_lanes(qa, w_qb, cos_q, sin_q, m_heads * nope // tq_n, rope // 2, BF16, tm_any, tq_n)
        scale = 1.0 / math.sqrt(nope + rope)
        o = _attn_prompt(q, w_uk_h, lat_b, kr_b, w_uv_h, proj, q_lora, bp, t_p, scale,
                         min(4, m_heads), min(512, t_p))

        q_lat_s = _head_in(q, w_uk_h, mp, ms)
        q_lat_s = jnp.transpose(q_lat_s.reshape(m_heads, bs, t_s, lora), (1, 0, 2, 3)).reshape(bs, m_heads * t_s, lora)
        q_rope_s = jnp.transpose(q[mp:, m_heads * nope:].reshape(bs, t_s, m_heads, rope),
                                 (0, 2, 1, 3)).reshape(bs, m_heads * t_s, rope)
        o_s = _attn_sample(q_lat_s, q_rope_s, lat_new, kr_new, cache_kv_latent, cache_k_rope,
                           page_table, t_s, scale, 2, min(16, page_table.shape[1]))
        o_s = jnp.transpose(o_s.reshape(bs, m_heads, t_s, lora), (1, 0, 2, 3)).reshape(m_heads, ms, lora)
        o = _head_out(o_s, w_uv_h, proj, q_lora, o, mp)
        h = _mm_residual(o, mla_w_o[b], h, gt_p, gt_s, t_p, mp, tm, _col_tile(d))

    y_p, y_s = _rmsgain_split(h, final_norm_g, mp, tr)
    return (y_p.reshape(bp, t_p, d), y_s.reshape(bs, t_s, d),
            jnp.stack(ret_p),
            lat_p.reshape(bp, t_p, lora), kr_p.reshape(bp, t_p, rope),
            ret_s,
            lat_s.reshape(bs, t_s, lora), kr_s.reshape(bs, t_s, rope))
```

```python
import functools
import math

import jax
import jax.numpy as jnp
from jax import lax
from jax.experimental import pallas as pl
from jax.experimental.pallas import tpu as pltpu

EPS = 1e-6
LOG_ROPE_BASE = math.log(10000.0)
R_CHUNK = 128
VMEM_LIMIT_BYTES = 56 * 1024 * 1024
LANES = 128
BF16_SUBLANES = 16
PAGE_SLOTS = 3

F32 = jnp.float32
BF16 = jnp.bfloat16
NT_DIMS = (((1,), (1,)), ((), ()))


def _cparams(n_axes):
    return pltpu.CompilerParams(dimension_semantics=("arbitrary",) * n_axes,
                                vmem_limit_bytes=VMEM_LIMIT_BYTES)


def _silu(x):
    return x * (1.0 / (1.0 + jnp.exp(-x)))


def _rms(x):
    return x * lax.rsqrt(jnp.mean(x * x, axis=-1, keepdims=True) + EPS)


def _adaln_kernel(c_ref, w_ref, b_ref, o_ref):
    a = _silu(c_ref[...]).astype(BF16)
    o_ref[...] = jnp.dot(a, w_ref[...].astype(BF16), preferred_element_type=F32) + b_ref[...]


def _adaln(c, w, b, tn=512):
    mc, d = c.shape
    n = w.shape[1]
    tn = min(tn, n)
    return pl.pallas_call(
        _adaln_kernel,
        out_shape=jax.ShapeDtypeStruct((mc, n), F32),
        grid=(n // tn,),
        in_specs=[pl.BlockSpec((mc, d), lambda j: (0, 0)),
                  pl.BlockSpec((d, tn), lambda j: (0, j)),
                  pl.BlockSpec((1, tn), lambda j: (0, j))],
        out_specs=pl.BlockSpec((mc, tn), lambda j: (0, j)),
        compiler_params=_cparams(1),
        name="adaln",
    )(c, w, b.reshape(1, n))


def _normmod_kernel(nsets, n_ptiles, xp_ref, xs_ref, *refs):
    ins, outs = refs[:4 * nsets], refs[4 * nsets:]
    i = pl.program_id(0)

    @pl.when(i < n_ptiles)
    def _():
        xn = _rms(xp_ref[...])
        for s in range(nsets):
            shp, scp = ins[4 * s:4 * s + 2]
            outs[s][...] = (xn * (1.0 + scp[0]) + shp[0]).astype(outs[s].dtype)

    @pl.when(i >= n_ptiles)
    def _():
        xn = _rms(xs_ref[...])
        for s in range(nsets):
            shs, scs = ins[4 * s + 2:4 * s + 4]
            outs[s][...] = (xn * (1.0 + scs[...]) + shs[...]).astype(outs[s].dtype)


def _row_parts(x, n_prompt_rows, tile):
    xp, xs = x if isinstance(x, tuple) else (x, x)
    n_ptiles = n_prompt_rows // tile
    s0 = 0 if isinstance(x, tuple) else n_ptiles
    return xp, xs, (lambda i: jnp.minimum(i, n_ptiles - 1)), (lambda i: jnp.maximum(i - n_ptiles, 0) + s0)


def _normmod(x, mods, rows_per_seq, n_prompt_rows, tr):
    xp, xs, prow, srow = _row_parts(x, n_prompt_rows, tr)
    m, d = xp.shape[0] + (xs.shape[0] if isinstance(x, tuple) else 0), xp.shape[1]
    n_ptiles = n_prompt_rows // tr
    tiles_per_seq = rows_per_seq // tr
    n_seq = mods[0][0].shape[0]
    pspec = pl.BlockSpec((1, 1, d), lambda i: (jnp.minimum(i // tiles_per_seq, n_seq - 1), 0, 0))
    sspec = pl.BlockSpec((tr, d), lambda i: (jnp.maximum(i - n_ptiles, 0), 0))
    in_specs = [pl.BlockSpec((tr, d), lambda i: (prow(i), 0)), pl.BlockSpec((tr, d), lambda i: (srow(i), 0))]
    args = [xp, xs]
    for (shp, scp, shs, scs) in mods:
        in_specs += [pspec, pspec, sspec, sspec]
        args += [shp, scp, shs, scs]
    outs = pl.pallas_call(
        functools.partial(_normmod_kernel, len(mods), n_ptiles),
        out_shape=[jax.ShapeDtypeStruct((m, d), BF16)] * len(mods),
        grid=(m // tr,),
        in_specs=in_specs,
        out_specs=[pl.BlockSpec((tr, d), lambda i: (i, 0))] * len(mods),
        compiler_params=_cparams(1),
        name="normmod",
    )(*args)
    return outs


def _rmsgain_kernel(x_ref, g_ref, o_ref):
    o_ref[...] = (_rms(x_ref[...]) * g_ref[...]).astype(o_ref.dtype)


def _rmsgain(x, g, width, out_dtype, tr):
    m = x.shape[0]
    return pl.pallas_call(
        _rmsgain_kernel,
        out_shape=jax.ShapeDtypeStruct((m, width), out_dtype),
        grid=(m // tr,),
        in_specs=[pl.BlockSpec((tr, width), lambda i: (i, 0)),
                  pl.BlockSpec((1, width), lambda i: (0, 0))],
        out_specs=pl.BlockSpec((tr, width), lambda i: (i, 0)),
        compiler_params=_cparams(1),
        name="rmsgain",
    )(x, g.reshape(1, width))


def _rmsgain_split_kernel(n_ptiles, x_ref, g_ref, op_ref, os_ref):
    y = _rms(x_ref[...]) * g_ref[...]
    i = pl.program_id(0)

    @pl.when(i < n_ptiles)
    def _():
        op_ref[...] = y

    @pl.when(i >= n_ptiles)
    def _():
        os_ref[...] = y


def _rmsgain_split(x, g, n_prompt_rows, tr):
    m, d = x.shape
    n_ptiles = n_prompt_rows // tr
    return pl.pallas_call(
        functools.partial(_rmsgain_split_kernel, n_ptiles),
        out_shape=[jax.ShapeDtypeStruct((n_prompt_rows, d), F32),
                   jax.ShapeDtypeStruct((m - n_prompt_rows, d), F32)],
        grid=(m // tr,),
        in_specs=[pl.BlockSpec((tr, d), lambda i: (i, 0)),
                  pl.BlockSpec((1, d), lambda i: (0, 0))],
        out_specs=[pl.BlockSpec((tr, d), lambda i: (jnp.minimum(i, n_ptiles - 1), 0)),
                   pl.BlockSpec((tr, d), lambda i: (jnp.maximum(i - n_ptiles, 0), 0))],
        compiler_params=_cparams(1),
        name="rmsgain_split",
    )(x, g.reshape(1, d))


def _wdot(x_ref, w_ref, wb_ref):
    @pl.when(pl.program_id(1) == 0)
    def _():
        wb_ref[...] = w_ref[...].astype(BF16)

    return jnp.dot(x_ref[...], wb_ref[...], preferred_element_type=F32)


def _mm_plain_kernel(x_ref, w_ref, o_ref, wb_ref):
    o_ref[...] = _wdot(x_ref, w_ref, wb_ref).astype(o_ref.dtype)


def _mm_rope_pairs_kernel(n_q, n_rope, q_scale, half, x_ref, w_ref, cos_ref, sin_ref, o_ref, wb_ref):
    acc = _wdot(x_ref, w_ref, wb_ref)
    n = pl.program_id(0)
    tn = acc.shape[1]
    sc = jnp.where(n < n_q, q_scale, 1.0).astype(F32)
    c = jnp.where(n < n_rope, cos_ref[...], 1.0) * sc
    s = jnp.where(n < n_rope, sin_ref[...], 0.0) * sc
    for h in range(tn // (2 * half)):
        a, b = h * 2 * half, h * 2 * half + half
        x1, x2 = acc[:, a:b], acc[:, b:b + half]
        o_ref[:, a:b] = (x1 * c - x2 * s).astype(o_ref.dtype)
        o_ref[:, b:b + half] = (x2 * c + x1 * s).astype(o_ref.dtype)


def _mm_rope_lanes_kernel(n_plain, half, x_ref, w_ref, cos_ref, sin_ref, o_ref, wb_ref):
    acc = _wdot(x_ref, w_ref, wb_ref)
    n = pl.program_id(0)
    tm, tn = acc.shape

    c = jnp.where(n >= n_plain, cos_ref[...], 1.0)
    s = jnp.where(n >= n_plain, sin_ref[...], 0.0)
    lane = lax.broadcasted_iota(jnp.int32, (tm, LANES), 1)
    first = lax.rem(lane, 2 * half) < half
    for j in range(tn // LANES):
        x = acc[:, j * LANES:(j + 1) * LANES]
        rot = jnp.where(first, pltpu.roll(x, LANES - half, 1), pltpu.roll(x, half, 1))
        o_ref[:, j * LANES:(j + 1) * LANES] = (x * c + rot * s).astype(o_ref.dtype)


def _mm_residual_kernel(n_ptiles, x_ref, w_ref, resp_ref, ress_ref, gp_ref, gs_ref, o_ref, wb_ref):
    acc = _wdot(x_ref, w_ref, wb_ref)
    is_prompt = pl.program_id(1) < n_ptiles
    gate = jnp.where(is_prompt, gp_ref[0], gs_ref[...])
    o_ref[...] = jnp.where(is_prompt, resp_ref[...], ress_ref[...]) + gate * acc


def _mm_call(kernel_fn, x, w, extra_args, extra_specs, out_dtype, tm, tn, name):
    m, k = x.shape
    n = w.shape[1]
    single = 2 * k * tn * w.dtype.itemsize > VMEM_LIMIT_BYTES // 2
    w_spec = pl.BlockSpec((k, tn), lambda j, i: (0, j), **({"pipeline_mode": pl.Buffered(1)} if single else {}))
    return pl.pallas_call(
        kernel_fn,
        out_shape=jax.ShapeDtypeStruct((m, n), out_dtype),
        grid=(n // tn, m // tm),
        in_specs=[pl.BlockSpec((tm, k), lambda j, i: (i, 0)), w_spec] + extra_specs,
        out_specs=pl.BlockSpec((tm, tn), lambda j, i: (i, j)),
        scratch_shapes=[pltpu.VMEM((k, tn), BF16)],
        compiler_params=_cparams(2),
        name=name,
    )(x, w, *extra_args)


def _mm_plain(x, w, out_dtype, tm, tn):
    return _mm_call(_mm_plain_kernel, x, w, [], [], out_dtype, tm, tn, "mm_plain")


def _mm_rope_pairs(x, w, cos, sin, n_q, n_rope, q_scale, out_dtype, tm, tn):
    half = cos.shape[1]
    tspec = pl.BlockSpec((tm, half), lambda j, i: (i, 0))
    return _mm_call(functools.partial(_mm_rope_pairs_kernel, n_q, n_rope, q_scale, half),
                    x, w, [cos, sin], [tspec, tspec], out_dtype, tm, tn, "mm_rope_pairs")


def _mm_rope_lanes(x, w, cos, sin, n_plain, half, out_dtype, tm, tn):
    tspec = pl.BlockSpec((tm, LANES), lambda j, i: (i, 0))
    return _mm_call(functools.partial(_mm_rope_lanes_kernel, n_plain, half),
                    x, w, [cos, sin], [tspec, tspec], out_dtype, tm, tn, "mm_rope_lanes")


def _mm_residual(x, w, res, gate_p, gate_s, rows_per_seq, n_prompt_rows, tm, tn):
    n_ptiles = n_prompt_rows // tm
    tiles_per_seq = rows_per_seq // tm
    n_seq = gate_p.shape[0]
    res_p, res_s, prow, srow = _row_parts(res, n_prompt_rows, tm)
    specs = [pl.BlockSpec((tm, tn), lambda j, i: (prow(i), j)),
             pl.BlockSpec((tm, tn), lambda j, i: (srow(i), j)),
             pl.BlockSpec((1, 1, tn), lambda j, i: (jnp.minimum(i // tiles_per_seq, n_seq - 1), 0, j)),
             pl.BlockSpec((tm, tn), lambda j, i: (jnp.maximum(i - n_ptiles, 0), j))]
    return _mm_call(functools.partial(_mm_residual_kernel, n_ptiles),
                    x, w, [res_p, res_s, gate_p, gate_s], specs, F32, tm, tn, "mm_residual")


def _ret_prompt_kernel(n_chunks, hp, lg_ref, gc_ref, q_ref, k_ref, v_ref, g_ref, gn_ref,
                       o_ref, sout_ref, s_ref):
    hg, t = pl.program_id(1), pl.program_id(2)
    c = R_CHUNK
    dk, dv = s_ref.shape[1], s_ref.shape[2]

    @pl.when(t == 0)
    def _():
        s_ref[...] = jnp.zeros_like(s_ref)

    ri = lax.broadcasted_iota(jnp.int32, (c, c), 0)
    ci = lax.broadcasted_iota(jnp.int32, (c, c), 1)
    diff = (ri - ci).astype(F32)
    idx = lax.broadcasted_iota(jnp.int32, (c, 1), 0).astype(F32)
    lgs = [lg_ref[hg * hp + u] for u in range(hp)]
    gcs = [gc_ref[hg * hp + u] for u in range(hp)]
    decay = [jnp.where(diff >= 0, jnp.exp(lg * jnp.maximum(diff, 0.0)), 0.0) for lg in lgs]
    q_decay = [jnp.exp(lg * (idx + 1.0)) for lg in lgs]
    k_decay = [jnp.exp(lg * (c - 1.0 - idx)) for lg in lgs]

    def recur(j, u):
        rows = slice(j * c, (j + 1) * c)
        q, k = q_ref[rows, u * dk:(u + 1) * dk], k_ref[rows, u * dk:(u + 1) * dk]
        v = v_ref[rows, u * dv:(u + 1) * dv]
        state = s_ref[u]
        scores = lax.dot_general(q, k, NT_DIMS, preferred_element_type=F32) * decay[u]
        o = (jnp.dot(scores.astype(BF16), v, preferred_element_type=F32)
             + jnp.dot(q, state.astype(BF16), preferred_element_type=F32) * q_decay[u])
        kd_t = (k.astype(F32) * k_decay[u]).T.astype(BF16)
        s_ref[u] = gcs[u] * state + jnp.dot(kd_t, v, preferred_element_type=F32)
        return o

    for j in range(n_chunks):
        rows = slice(j * c, (j + 1) * c)
        outs = [recur(j, u) for u in range(hp)]
        for u in range(hp):
            cols = slice(u * dv, (u + 1) * dv)
            gate = _silu(g_ref[rows, cols].astype(F32))
            o_ref[rows, cols] = (_rms(outs[u]) * gn_ref[:, cols] * gate).astype(o_ref.dtype)

    @pl.when(t == pl.num_programs(2) - 1)
    def _():
        sout_ref[0] = s_ref[...]


def _ret_prompt(proj, gn_g, log_g, n_seq, seq_len, heads, dk, dv, tb, hp):
    m = proj.shape[0]
    nt = seq_len // tb
    hgs = heads // hp
    gc = jnp.exp(log_g * R_CHUNK)
    vcol = (2 * heads * dk) // (hp * dv)
    assert heads % hp == 0 and (2 * heads * dk) % (hp * dv) == 0

    def rowmap(col0):
        return lambda b, hg, t, *_: (b * nt + t, col0 + hg)

    grid_spec = pltpu.PrefetchScalarGridSpec(
        num_scalar_prefetch=2,
        grid=(n_seq, hgs, nt),
        in_specs=[pl.BlockSpec((tb, hp * dk), rowmap(0)),
                  pl.BlockSpec((tb, hp * dk), rowmap(hgs)),
                  pl.BlockSpec((tb, hp * dv), rowmap(vcol)),
                  pl.BlockSpec((tb, hp * dv), rowmap(vcol + hgs)),
                  pl.BlockSpec((1, hp * dv), lambda b, hg, t, *_: (0, hg))],
        out_specs=[pl.BlockSpec((tb, hp * dv), rowmap(0)),
                   pl.BlockSpec((1, hp, dk, dv), lambda b, hg, t, *_: (b, hg, 0, 0))],
        scratch_shapes=[pltpu.VMEM((hp, dk, dv), F32)],
    )
    return pl.pallas_call(
        functools.partial(_ret_prompt_kernel, tb // R_CHUNK, hp),
        out_shape=[jax.ShapeDtypeStruct((m, heads * dv), BF16),
                   jax.ShapeDtypeStruct((n_seq, heads, dk, dv), F32)],
        grid_spec=grid_spec,
        compiler_params=_cparams(3),
        name="ret_prompt",
    )(log_g, gc, proj, proj, proj, proj, gn_g.reshape(1, heads * dv))


def _ret_sample_kernel(bb, t_len, lg_ref, gc_ref, q_ref, k_ref, v_ref, g_ref, gn_ref, s_ref, *refs):
    o_ref, sout_ref = refs[-2:]
    h = pl.program_id(1)
    lg, gc = lg_ref[h], gc_ref[h]
    r = bb * t_len
    ri = lax.broadcasted_iota(jnp.int32, (r, r), 0)
    ci = lax.broadcasted_iota(jnp.int32, (r, r), 1)
    diff = (ri - ci).astype(F32)
    same_seq = (ri // t_len) == (ci // t_len)
    decay = jnp.where(same_seq & (diff >= 0), jnp.exp(lg * jnp.maximum(diff, 0.0)), 0.0)
    pos = lax.rem(lax.broadcasted_iota(jnp.int32, (r, 1), 0), t_len).astype(F32)
    q_decay = jnp.exp(lg * (pos + 1.0))
    k_decay = jnp.exp(lg * (t_len - 1.0 - pos))
    row_seq = lax.broadcasted_iota(jnp.int32, (r, 1), 0) // t_len
    col_seq = lax.broadcasted_iota(jnp.int32, (1, r), 1) // t_len

    q, k, v = q_ref[...], k_ref[...], v_ref[...]
    scores = lax.dot_general(q, k, NT_DIMS, preferred_element_type=F32) * decay
    o = jnp.dot(scores.astype(BF16), v, preferred_element_type=F32)
    kd_t = (k.astype(F32) * k_decay).T
    cross = jnp.zeros_like(o)
    for b in range(bb):
        state = s_ref[b, 0]
        cross_b = jnp.dot(q, state.astype(BF16), preferred_element_type=F32)
        cross = jnp.where(row_seq == b, cross_b, cross)
        kd_b = jnp.where(col_seq == b, kd_t, 0.0).astype(BF16)
        sout_ref[b, 0] = (gc * state + jnp.dot(kd_b, v, preferred_element_type=F32)).astype(sout_ref.dtype)
    o = o + cross * q_decay
    o_ref[...] = (_rms(o) * gn_ref[...] * _silu(g_ref[...].astype(F32))).astype(o_ref.dtype)


def _ret_sample(proj, o_all, states, new_states, layer, gn_g, log_g, n_prompt_rows, t_len, heads, dk, dv, bb):
    n_seq = states.shape[1]
    r = bb * t_len
    row0 = n_prompt_rows // r
    gc = jnp.exp(log_g * t_len)
    kcol, vcol, gcol = heads, (2 * heads * dk) // dv, (2 * heads * dk) // dv + heads

    def rowmap(col0):
        return lambda i, h, *_: (row0 + i, col0 + h)

    sspec = pl.BlockSpec((None, bb, 1, dk, dv), lambda i, h, *_: (layer, i, h, 0, 0))
    any_spec = pl.BlockSpec(memory_space=pl.ANY)
    args = [log_g, gc, proj, proj, proj, proj, gn_g.reshape(1, heads * dv), states, o_all]
    aliases = {8: 0}
    if new_states is not None:
        args.append(new_states)
        aliases[9] = 1
    grid_spec = pltpu.PrefetchScalarGridSpec(
        num_scalar_prefetch=2,
        grid=(n_seq // bb, heads),
        in_specs=[pl.BlockSpec((r, dk), rowmap(0)),
                  pl.BlockSpec((r, dk), rowmap(kcol)),
                  pl.BlockSpec((r, dv), rowmap(vcol)),
                  pl.BlockSpec((r, dv), rowmap(gcol)),
                  pl.BlockSpec((1, dv), lambda i, h, *_: (0, h)),
                  sspec] + [any_spec] * (len(args) - 8),
        out_specs=[pl.BlockSpec((r, dv), rowmap(0)), sspec],
    )
    return pl.pallas_call(
        functools.partial(_ret_sample_kernel, bb, t_len),
        out_shape=[jax.ShapeDtypeStruct(o_all.shape, o_all.dtype),
                   jax.ShapeDtypeStruct(states.shape, states.dtype)],
        grid_spec=grid_spec,
        input_output_aliases=aliases,
        compiler_params=_cparams(2),
        name="ret_sample",
    )(*args)


def _kv_down_kernel(n_ptiles, x_ref, wl_ref, wr_ref, ng_ref, cos_ref, sin_ref,
                    latp_ref, krp_ref, lats_ref, krs_ref, latb_ref, krb_ref, wlb_ref, wrb_ref):
    i = pl.program_id(0)

    @pl.when(i == 0)
    def _():
        wlb_ref[...] = wl_ref[...].astype(BF16)
        wrb_ref[...] = wr_ref[...].astype(BF16)

    x = x_ref[...]
    lat = _rms(jnp.dot(x, wlb_ref[...], preferred_element_type=F32)) * ng_ref[...]
    c = jnp.dot(x, wrb_ref[...], preferred_element_type=F32)
    half = c.shape[1] // 2
    rot = jnp.concatenate([c[:, half:], c[:, :half]], axis=-1)
    kr = c * cos_ref[...] + rot * sin_ref[...]
    latb_ref[...] = lat.astype(BF16)
    krb_ref[...] = kr.astype(BF16)

    @pl.when(i < n_ptiles)
    def _():
        latp_ref[...] = lat
        krp_ref[...] = kr

    @pl.when(i >= n_ptiles)
    def _():
        lats_ref[...] = lat
        krs_ref[...] = kr


def _kv_down(xn, w_down, norm_g, cos, sin, lora, n_prompt_rows, tm):
    m, d = xn.shape
    rope = w_down.shape[1] - lora
    n_ptiles = n_prompt_rows // tm
    w_lat, w_rope = w_down[:, :lora], w_down[:, lora:]
    row = lambda i: (i, 0)
    fixed = lambda i: (0, 0)
    prow = lambda i: (jnp.minimum(i, n_ptiles - 1), 0)
    srow = lambda i: (jnp.maximum(i - n_ptiles, 0), 0)
    ms = m - n_prompt_rows
    return pl.pallas_call(
        functools.partial(_kv_down_kernel, n_ptiles),
        out_shape=[jax.ShapeDtypeStruct((n_prompt_rows, lora), F32), jax.ShapeDtypeStruct((n_prompt_rows, rope), F32),
                   jax.ShapeDtypeStruct((ms, lora), F32), jax.ShapeDtypeStruct((ms, rope), F32),
                   jax.ShapeDtypeStruct((m, lora), BF16), jax.ShapeDtypeStruct((m, rope), BF16)],
        grid=(m // tm,),
        in_specs=[pl.BlockSpec((tm, d), row), pl.BlockSpec((d, lora), fixed), pl.BlockSpec((d, rope), fixed),
                  pl.BlockSpec((1, lora), fixed), pl.BlockSpec((tm, rope), row), pl.BlockSpec((tm, rope), row)],
        out_specs=[pl.BlockSpec((tm, lora), prow), pl.BlockSpec((tm, rope), prow),
                   pl.BlockSpec((tm, lora), srow), pl.BlockSpec((tm, rope), srow),
                   pl.BlockSpec((tm, lora), row), pl.BlockSpec((tm, rope), row)],
        scratch_shapes=[pltpu.VMEM((d, lora), BF16), pltpu.VMEM((d, rope), BF16)],
        compiler_params=_cparams(1),
        name="kv_down",
    )(xn, w_lat, w_rope, norm_g.reshape(1, lora), cos, sin)


def _head_in_kernel(x_ref, w_ref, o_ref):
    o_ref[0] = jnp.dot(x_ref[...], w_ref[0].astype(BF16), preferred_element_type=F32).astype(o_ref.dtype)


def _head_in(x, w, row0, rows):
    heads, kin, nout = w.shape
    rb = row0 // rows
    return pl.pallas_call(
        _head_in_kernel,
        out_shape=jax.ShapeDtypeStruct((heads, rows, nout), BF16),
        grid=(heads,),
        in_specs=[pl.BlockSpec((rows, kin), lambda h: (rb, h)),
                  pl.BlockSpec((1, kin, nout), lambda h: (h, 0, 0))],
        out_specs=pl.BlockSpec((1, rows, nout), lambda h: (h, 0, 0)),
        compiler_params=_cparams(1),
        name="head_in",
    )(x, w)


def _head_out_kernel(x_ref, w_ref, g_ref, oin_ref, o_ref):
    del oin_ref
    o = jnp.dot(x_ref[0], w_ref[0].astype(BF16), preferred_element_type=F32)
    o_ref[...] = (o * _silu(g_ref[...])).astype(o_ref.dtype)


def _head_out(x, w, g, g_col0, o_all, row0):
    heads, rows, kin = x.shape
    nout = w.shape[2]
    gblk = g_col0 // nout
    rb = row0 // rows
    return pl.pallas_call(
        _head_out_kernel,
        out_shape=jax.ShapeDtypeStruct(o_all.shape, o_all.dtype),
        grid=(heads,),
        in_specs=[pl.BlockSpec((1, rows, kin), lambda h: (h, 0, 0)),
                  pl.BlockSpec((1, kin, nout), lambda h: (h, 0, 0)),
                  pl.BlockSpec((rows, nout), lambda h: (rb, gblk + h)),
                  pl.BlockSpec(memory_space=pl.ANY)],
        out_specs=pl.BlockSpec((rows, nout), lambda h: (rb, h)),
        input_output_aliases={3: 0},
        compiler_params=_cparams(1),
        name="head_out",
    )(x, w, g, o_all)


def _lanes(x, n):
    return jnp.tile(x, (1, n // LANES)) if n >= LANES else x[:, :n]


def _softmax_update(s, v, c, m_ref, l_ref, acc_ref):
    m_prev = m_ref[...]
    m_new = jnp.maximum(m_prev, jnp.max(s, axis=-1, keepdims=True))
    p = jnp.exp2((s - _lanes(m_new, s.shape[1])) * c)
    alpha = jnp.exp2((m_prev - m_new) * c)
    l_ref[...] = alpha * l_ref[...] + jnp.sum(p, axis=-1, keepdims=True)
    acc_ref[...] = (_lanes(alpha, acc_ref.shape[1]) * acc_ref[...]
                    + jnp.dot(p.astype(BF16), v, preferred_element_type=F32))
    m_ref[...] = m_new


def _softmax_init(m_ref, l_ref, acc_ref):
    m_ref[...] = jnp.full_like(m_ref, -jnp.inf)
    l_ref[...] = jnp.zeros_like(l_ref)
    acc_ref[...] = jnp.zeros_like(acc_ref)


def _attn_prompt_kernel(g, c, qi_ref, ki_ref, qn_ref, qr_ref, wuk_ref, k_ref, kr_ref, wuv_ref, gate_ref,
                        o_ref, ql_s, qr_s, m_s, l_s, acc_s):
    t = pl.program_id(2)
    qi, ki = qi_ref[t], ki_ref[t]
    tq = qn_ref.shape[0]
    nope, rope, dv = qn_ref.shape[1] // g, qr_ref.shape[1] // g, o_ref.shape[1] // g
    tk = k_ref.shape[0]

    @pl.when(ki == 0)
    def _():
        for h in range(g):
            rows = pl.ds(h * tq, tq)
            ql_s[rows, :] = jnp.dot(qn_ref[:, h * nope:(h + 1) * nope], wuk_ref[h].astype(BF16),
                                    preferred_element_type=F32).astype(BF16)
            qr_s[rows, :] = qr_ref[:, h * rope:(h + 1) * rope]
        _softmax_init(m_s, l_s, acc_s)

    def update(masked):
        k, kr = k_ref[...], kr_ref[...]
        if masked:
            visible = (lax.broadcasted_iota(jnp.int32, (tq, tk), 1)
                       <= lax.broadcasted_iota(jnp.int32, (tq, tk), 0))
        def scores(h):
            rows = pl.ds(h * tq, tq)
            s = (lax.dot_general(ql_s[rows, :], k, NT_DIMS, preferred_element_type=F32)
                 + lax.dot_general(qr_s[rows, :], kr, NT_DIMS, preferred_element_type=F32))
            return jnp.where(visible, s, -jnp.inf) if masked else s

        s_next = scores(0)
        for h in range(g):
            s, s_next = s_next, (scores(h + 1) if h + 1 < g else None)
            rows = pl.ds(h * tq, tq)
            _softmax_update(s, k, c, m_s.at[rows, :], l_s.at[rows, :], acc_s.at[rows, :])

    @pl.when(ki < qi)
    def _():
        update(False)

    @pl.when(ki == qi)
    def _():
        update(True)
        for h in range(g):
            rows = pl.ds(h * tq, tq)
            o_lat = acc_s[rows, :] * _lanes(1.0 / l_s[rows, :], acc_s.shape[1])
            o = jnp.dot(o_lat.astype(BF16), wuv_ref[h].astype(BF16), preferred_element_type=F32)
            cols = slice(h * dv, (h + 1) * dv)
            o_ref[:, cols] = (o * _silu(gate_ref[:, cols])).astype(o_ref.dtype)


def _attn_prompt(q, w_uk_h, lat, kr, w_uv_h, gate, gate_col0, n_seq, seq_len, scale, g, tq):
    m = q.shape[0]
    heads, nope, lora = w_uk_h.shape
    dv = w_uv_h.shape[2]
    rope = kr.shape[1]
    nq = seq_len // tq
    pairs = [(i, j) for i in range(nq) for j in range(i + 1)]
    qi_tab = jnp.asarray([p[0] for p in pairs], jnp.int32)
    ki_tab = jnp.asarray([p[1] for p in pairs], jnp.int32)
    rope_blk0 = heads * nope // (g * rope)
    gate_blk0 = gate_col0 // (g * dv)
    assert heads * nope % (g * rope) == 0 and gate_col0 % (g * dv) == 0

    def qrow(col0):
        return lambda b, hg, t, qi, ki: (b * nq + qi[t], col0 + hg)

    krow = lambda b, hg, t, qi, ki: (b * nq + ki[t], 0)
    hmap = lambda b, hg, t, qi, ki: (hg, 0, 0)
    grid_spec = pltpu.PrefetchScalarGridSpec(
        num_scalar_prefetch=2,
        grid=(n_seq, heads // g, len(pairs)),
        in_specs=[pl.BlockSpec((tq, g * nope), qrow(0)),
                  pl.BlockSpec((tq, g * rope), qrow(rope_blk0)),
                  pl.BlockSpec((g, nope, lora), hmap),
                  pl.BlockSpec((tq, lora), krow),
                  pl.BlockSpec((tq, rope), krow),
                  pl.BlockSpec((g, lora, dv), hmap),
                  pl.BlockSpec((tq, g * dv), qrow(gate_blk0))],
        out_specs=pl.BlockSpec((tq, g * dv), qrow(0)),
        scratch_shapes=[pltpu.VMEM((g * tq, lora), BF16), pltpu.VMEM((g * tq, rope), BF16),
                        pltpu.VMEM((g * tq, LANES), F32), pltpu.VMEM((g * tq, LANES), F32),
                        pltpu.VMEM((g * tq, lora), F32)],
    )
    return pl.pallas_call(
        functools.partial(_attn_prompt_kernel, g, scale * math.log2(math.e)),
        out_shape=jax.ShapeDtypeStruct((m, heads * dv), BF16),
        grid_spec=grid_spec,
        compiler_params=_cparams(3),
        name="attn_prompt",
    )(qi_tab, ki_tab, q, q, w_uk_h, lat, kr, w_uv_h, gate)


def _attn_sample_kernel(ns, pg, t_len, c, pt_ref, ql_ref, qr_ref, kn_ref, krn_ref, lat_hbm, krt_hbm,
                        o_ref, lat_buf, krt_buf, lat_sem, krt_sem, kb_s, krb_s, m_s, l_s, acc_s):
    i, j = pl.program_id(0), pl.program_id(1)
    n_steps = pl.num_programs(1)
    last = pl.num_programs(0) * n_steps - 1
    g = i * n_steps + j
    n_slots = lat_buf.shape[0]
    ahead = n_slots - 1
    slot = lax.rem(g, n_slots)
    page = lat_buf.shape[2]

    def page_copies(step, into):
        seq0, first = (step // n_steps) * ns, lax.rem(step, n_steps) * pg
        out = []
        for b in range(ns):
            for p in range(pg):
                pid = pt_ref[(seq0 + b) * (n_steps * pg) + first + p]
                out.append(pltpu.make_async_copy(lat_hbm.at[pid], lat_buf.at[into, b * pg + p], lat_sem.at[into]))
                out.append(pltpu.make_async_copy(krt_hbm.at[pid], krt_buf.at[into, b * pg + p], krt_sem.at[into]))
        return out

    @pl.when(j == 0)
    def _():
        _softmax_init(m_s, l_s, acc_s)

    @pl.when(g == 0)
    def _():
        for d in range(ahead):
            for cp in page_copies(jnp.minimum(d, last), d):
                cp.start()

    for cp in page_copies(g, slot):
        cp.wait()
    for cp in page_copies(jnp.minimum(g + ahead, last), lax.rem(g + ahead, n_slots)):
        cp.start()

    for b in range(ns):
        for p in range(pg):
            kb_s[b, p * page:(p + 1) * page, :] = lat_buf[slot, b * pg + p].astype(BF16)
            krb_s[b, :, p * page:(p + 1) * page] = krt_buf[slot, b * pg + p].astype(BF16)
    scores = [lax.dot_general(ql_ref[b], kb_s[b], NT_DIMS, preferred_element_type=F32)
              + jnp.dot(qr_ref[b], krb_s[b], preferred_element_type=F32) for b in range(ns)]
    for b in range(ns):
        _softmax_update(scores[b], kb_s[b], c, m_s.at[b], l_s.at[b], acc_s.at[b])

    @pl.when(g == last)
    def _():
        for d in range(1, n_slots):
            for cp in page_copies(last, lax.rem(g + d, n_slots)):
                cp.wait()

    @pl.when(j == n_steps - 1)
    def _():
        rows, nk = ql_ref.shape[1], kn_ref.shape[1]
        tok = lax.rem(lax.broadcasted_iota(jnp.int32, (rows, nk), 0), t_len)
        visible = lax.broadcasted_iota(jnp.int32, (rows, nk), 1) <= tok
        for b in range(ns):
            kn, krn = kn_ref[b], krn_ref[b]
            s = (lax.dot_general(ql_ref[b], kn, NT_DIMS, preferred_element_type=F32)
                 + lax.dot_general(qr_ref[b], krn, NT_DIMS, preferred_element_type=F32))
            _softmax_update(jnp.where(visible, s, -jnp.inf), kn, c, m_s.at[b], l_s.at[b], acc_s.at[b])
            o_ref[b] = (acc_s[b] * _lanes(1.0 / l_s[b], acc_s.shape[2])).astype(o_ref.dtype)


def _attn_sample(q_lat, q_rope, lat_new, kr_new, cache_lat, cache_kr, page_table, t_len, scale, ns, pg):
    n_seq, rows, lora = q_lat.shape
    rope = q_rope.shape[2]
    page = cache_lat.shape[1]
    n_pages = page_table.shape[1]
    nk = lat_new.shape[1]
    seqmap = lambda i, j, pt: (i, 0, 0)
    hbm = pl.BlockSpec(memory_space=pl.ANY)
    cache_krt = jnp.swapaxes(cache_kr, 1, 2)
    grid_spec = pltpu.PrefetchScalarGridSpec(
        num_scalar_prefetch=1,
        grid=(n_seq // ns, n_pages // pg),
        in_specs=[pl.BlockSpec((ns, rows, lora), seqmap), pl.BlockSpec((ns, rows, rope), seqmap),
                  pl.BlockSpec((ns, nk, lora), seqmap), pl.BlockSpec((ns, nk, rope), seqmap), hbm, hbm],
        out_specs=pl.BlockSpec((ns, rows, lora), seqmap),
        scratch_shapes=[pltpu.VMEM((PAGE_SLOTS, ns * pg, page, lora), cache_lat.dtype),
                        pltpu.VMEM((PAGE_SLOTS, ns * pg, rope, page), cache_kr.dtype),
                        pltpu.SemaphoreType.DMA((PAGE_SLOTS,)), pltpu.SemaphoreType.DMA((PAGE_SLOTS,)),
                        pltpu.VMEM((ns, pg * page, lora), BF16), pltpu.VMEM((ns, rope, pg * page), BF16),
                        pltpu.VMEM((ns, rows, LANES), F32), pltpu.VMEM((ns, rows, LANES), F32),
                        pltpu.VMEM((ns, rows, lora), F32)],
    )
    return pl.pallas_call(
        functools.partial(_attn_sample_kernel, ns, pg, t_len, scale * math.log2(math.e)),
        out_shape=jax.ShapeDtypeStruct((n_seq, rows, lora), BF16),
        grid_spec=grid_spec,
        compiler_params=_cparams(2),
        name="attn_sample",
    )(page_table.reshape(-1), q_lat, q_rope, lat_new, kr_new, cache_lat, cache_krt)


def _rope_angles(pos, half):
    inv = jnp.exp(-LOG_ROPE_BASE * jnp.arange(half, dtype=F32) / half)
    ang = pos[:, None] * inv[None, :]
    return jnp.cos(ang), jnp.sin(ang)


def _col_tile(*widths, pref=512):
    t = pref
    while any(w % t for w in widths):
        t //= 2
    assert t % LANES == 0
    return t


def _row_tile(m, pref):
    return max(t for t in range(BF16_SUBLANES, pref + 1, BF16_SUBLANES) if m % t == 0)


def _split_mod(mod, n, n_prompt_seq, n_sample_seq, t_sample):
    d = mod.shape[1] // n
    out = []
    for i in range(n):
        v = mod[:, i * d:(i + 1) * d]
        vp = v[:n_prompt_seq].reshape(n_prompt_seq, 1, d)
        vs = jnp.repeat(v[n_prompt_seq:n_prompt_seq + n_sample_seq], t_sample, axis=0)
        out.append((vp, vs))
    return out


def kernel(x_prompt, x_sample, c_prompt, c_sample, state_retention, cache_kv_latent, cache_k_rope, page_table, ret_w_mod, ret_b_mod, ret_w_in, ret_gn_g, ret_w_out, kv_w_mod, kv_b_mod, kv_w_down, kv_norm_g, kv_w_uk, kv_w_uv, mla_w_mod, mla_b_mod, mla_w_in, mla_q_norm_g, mla_w_qb, mla_w_o, final_norm_g):
    bp, t_p, d = x_prompt.shape
    bs, t_s, _ = x_sample.shape
    n_ret, _, r_heads, r_dk, r_dv = state_retention.shape
    n_mla = mla_w_mod.shape[0]
    lora, m_heads, nope = kv_w_uk.shape
    m_dv = kv_w_uv.shape[2]
    rope = cache_k_rope.shape[2]
    q_lora = mla_q_norm_g.shape[1]
    page = cache_kv_latent.shape[1]
    past_len = page_table.shape[1] * page
    mp, ms = bp * t_p, bs * t_s
    m = mp + ms

    tm = ms
    assert t_p % tm == 0 and tm % BF16_SUBLANES == 0
    tm_any = _row_tile(m, 1100)
    tr = min(256, tm)
    tn = 512
    tb = min(512, t_p)
    assert t_p % tb == 0 and tb % R_CHUNK == 0
    bb = 8
    assert bs % bb == 0 and (bb * t_s) % BF16_SUBLANES == 0 and mp % (bb * t_s) == 0
    assert r_dk == 2 * LANES and r_dv == 2 * r_dk and tn % r_dv == 0
    assert rope * 2 == LANES and nope == LANES and m_dv == LANES

    pos = jnp.concatenate([jnp.tile(jnp.arange(t_p, dtype=F32), bp),
                           jnp.tile(past_len + jnp.arange(t_s, dtype=F32), bs)])
    cos_r, sin_r = _rope_angles(pos, r_dk // 2)
    cos_m, sin_m = _rope_angles(pos, rope // 2)
    cos_kr = jnp.concatenate([cos_m, cos_m], axis=1)
    sin_kr = jnp.concatenate([-sin_m, sin_m], axis=1)
    cos_q = jnp.concatenate([cos_kr, cos_kr], axis=1)
    sin_q = jnp.concatenate([sin_kr, sin_kr], axis=1)

    log_g = jnp.log1p(-jnp.exp2(-5.0 - jnp.arange(r_heads, dtype=F32)))

    n_cond = bp + bs
    c_all = jnp.concatenate([c_prompt, c_sample], axis=0)
    c_all = jnp.pad(c_all, ((0, (-n_cond) % BF16_SUBLANES), (0, 0)))
    h = (x_prompt.reshape(mp, d), x_sample.reshape(ms, d))

    split = functools.partial(_split_mod, n_prompt_seq=bp, n_sample_seq=bs, t_sample=t_s)
    ret_p, ret_s = [], None
    for a in range(n_ret):
        (sh_p, sh_s), (sc_p, sc_s), (gt_p, gt_s) = split(_adaln(c_all, ret_w_mod[a], ret_b_mod[a]), 3)
        (xn,) = _normmod(h, [(sh_p, sc_p, sh_s, sc_s)], t_p, mp, tr)
        hk = r_heads * r_dk
        proj = _mm_rope_pairs(xn, ret_w_in[a], cos_r, sin_r, hk // tn, 2 * hk // tn, r_dk ** -0.5, BF16, tm_any, tn)
        o_all, s_p = _ret_prompt(proj, ret_gn_g[a], log_g, bp, t_p, r_heads, r_dk, r_dv, tb,
                                 2 if r_heads % 2 == 0 else 1)
        o_all, ret_s = _ret_sample(proj, o_all, state_retention, ret_s, a, ret_gn_g[a], log_g, mp, t_s,
                                   r_heads, r_dk, r_dv, bb)
        h = _mm_residual(o_all, ret_w_out[a], h, gt_p, gt_s, t_p, mp, tm, _col_tile(d))
        ret_p.append(s_p)

    lat_p = kr_p = lat_s = kr_s = None
    for b in range(n_mla):
        (sh_p, sh_s), (sc_p, sc_s), (gt_p, gt_s) = split(_adaln(c_all, mla_w_mod[b], mla_b_mod[b]), 3)
        if b == 0:
            (ksh_p, ksh_s), (ksc_p, ksc_s) = split(_adaln(c_all, kv_w_mod, kv_b_mod), 2)
            xn, xn_kv = _normmod(h, [(sh_p, sc_p, sh_s, sc_s), (ksh_p, ksc_p, ksh_s, ksc_s)], t_p, mp, tr // 2)
            lat_p, kr_p, lat_s, kr_s, lat_b, kr_b = _kv_down(xn_kv, kv_w_down, kv_norm_g, cos_kr, sin_kr,
                                                             lora, mp, tm)
            pad = ((0, 0), (0, BF16_SUBLANES - t_s), (0, 0))
            lat_new = jnp.pad(lat_b[mp:].reshape(bs, t_s, lora), pad)
            kr_new = jnp.pad(kr_b[mp:].reshape(bs, t_s, rope), pad)
            w_uk_h = jnp.transpose(kv_w_uk, (1, 2, 0))
            w_uv_h = jnp.transpose(kv_w_uv, (1, 0, 2))
        else:
            (xn,) = _normmod(h, [(sh_p, sc_p, sh_s, sc_s)], t_p, mp, tr)
        proj = _mm_plain(xn, mla_w_in[b], F32, tm_any, _col_tile(mla_w_in.shape[2]))
        qa = _rmsgain(proj, mla_q_norm_g[b], q_lora, BF16, tr)
        w_qb = mla_w_qb[b].reshape(q_lora, m_heads, nope + rope)
        w_qb = jnp.concatenate([w_qb[:, :, :nope].reshape(q_lora, m_heads * nope),
                                w_qb[:, :, nope:].reshape(q_lora, m_heads * rope)], axis=1)
        tq_n = _col_tile(m_heads * nope, m_heads * rope)
        q = _mm_rope_lanes(qa, w_qb, cos_q, sin_q, m_heads * nope // tq_n, rope // 2, BF16, tm_any, tq_n)
        scale = 1.0 / math.sqrt(nope + rope)
        o = _attn_prompt(q, w_uk_h, lat_b, kr_b, w_uv_h, proj, q_lora, bp, t_p, scale,
                         min(4, m_heads), min(512, t_p))

        q_lat_s = _head_in(q, w_uk_h, mp, ms)
        q_lat_s = jnp.transpose(q_lat_s.reshape(m_heads, bs, t_s, lora), (1, 0, 2, 3)).reshape(bs, m_heads * t_s, lora)
        q_rope_s = jnp.transpose(q[mp:, m_heads * nope:].reshape(bs, t_s, m_heads, rope),
                                 (0, 2, 1, 3)).reshape(bs, m_heads * t_s, rope)
        o_s = _attn_sample(q_lat_s, q_rope_s, lat_new, kr_new, cache_kv_latent, cache_k_rope,
                           page_table, t_s, scale, 2, min(16, page_table.shape[1]))
        o_s = jnp.transpose(o_s.reshape(bs, m_heads, t_s, lora), (1, 0, 2, 3)).reshape(m_heads, ms, lora)
        o = _head_out(o_s, w_uv_h, proj, q_lora, o, mp)
        h = _mm_residual(o, mla_w_o[b], h, gt_p, gt_s, t_p, mp, tm, _col_tile(d))

    y_p, y_s = _rmsgain_split(h, final_norm_g, mp, tr)
    return (y_p.reshape(bp, t_p, d), y_s.reshape(bs, t_s, d),
            jnp.stack(ret_p),
            lat_p.reshape(bp, t_p, lora), kr_p.reshape(bp, t_p, rope),
            ret_s,
            lat_s.reshape(bs, t_s, lora), kr_s.reshape(bs, t_s, rope))
```
